```python
import jax, jax.numpy as jnp
from jax import lax
import numpy as np

D_MODEL = 1024
BATCH = 4
SEQ = 4096
DEPTH = 1

CHUNK = 64
N_META = 16
EPS = 1e-6
NEG = -1e30
ML_HEADS = 4
ML_DQK = 128
ML_DV = 256
ML_CONV = 4
GATE_CAP = 15.0
RT_HEADS = 4
RT_DQK = 128
RT_DV = 256
ROPE_BASE = 10000.0
D_MIX = ML_HEADS * ML_DV + RT_HEADS * RT_DV
COLUMN_SIZES = (
    ML_HEADS * ML_DQK,
    ML_HEADS * ML_DQK,
    ML_HEADS * ML_DV,
    ML_HEADS * ML_DV,
    ML_HEADS,
    ML_HEADS,
    RT_HEADS * RT_DQK,
    RT_HEADS * RT_DQK,
    RT_HEADS * RT_DV,
    RT_HEADS * RT_DV,
)
D_IN = sum(COLUMN_SIZES)
D_FF = 2816
FFN_CONV = 3

kernel_name = "hymba_mlstm_retention_convffn"


def rmsnorm(x, g):
    xf = x.astype(jnp.float32)
    y = xf * lax.rsqrt(jnp.mean(xf * xf, axis=-1, keepdims=True) + EPS)
    return (y * g.astype(jnp.float32)).astype(x.dtype)


def head_norm(h, g):
    mu = jnp.mean(h, axis=-1, keepdims=True)
    var = jnp.mean(jnp.square(h - mu), axis=-1, keepdims=True)
    y = (h - mu) * lax.rsqrt(var + EPS)
    return y * g.astype(jnp.float32).reshape(h.shape[-2], h.shape[-1])


def causal_dwconv(x, w):
    k_w, c = w.shape
    return lax.conv_general_dilated(
        x, w.astype(x.dtype)[:, None, :], window_strides=(1,), padding=[(k_w - 1, 0)],
        dimension_numbers=("NWC", "WIO", "NWC"), feature_group_count=c)


def rotary(t, pos):
    d = t.shape[-1]
    inv_freq = ROPE_BASE ** (-jnp.arange(0, d, 2, dtype=jnp.float32) / d)
    ang = pos[:, None] * inv_freq[None, :]
    cos, sin = jnp.cos(ang)[None, :, None, :], jnp.sin(ang)[None, :, None, :]
    t1, t2 = t[..., : d // 2], t[..., d // 2:]
    return jnp.concatenate([t1 * cos - t2 * sin, t1 * sin + t2 * cos], axis=-1)


def to_chunks(t):
    b, lp, hh, d = t.shape
    return t.reshape(b, lp // CHUNK, CHUNK, hh, d).transpose(1, 0, 3, 2, 4)


def from_chunks(t):
    nc, b, hh, c, d = t.shape
    return t.transpose(1, 0, 3, 2, 4).reshape(b, nc * c, hh, d)


def mlstm_chunkwise(q, k, v, li, lf):
    _, b, hh, _, dqk = q.shape
    dv = v.shape[-1]
    causal = jnp.tril(jnp.ones((CHUNK, CHUNK), dtype=bool))

    def step(carry, inp):
        c_st, n_st, m_st = carry
        qc, kc, vc, lic, lfc = inp
        bcum = jnp.cumsum(lfc, axis=-1)
        log_d = bcum[..., :, None] - bcum[..., None, :] + lic[..., None, :]
        log_d = jnp.where(causal, log_d, -jnp.inf)
        log_inter = bcum + m_st[..., None]
        m_row = jnp.maximum(log_inter, jnp.max(log_d, axis=-1))
        w_intra = jnp.exp(log_d - m_row[..., None])
        w_inter = jnp.exp(log_inter - m_row)
        s = jnp.einsum("bhid,bhjd->bhij", qc, kc) * w_intra
        num = (jnp.einsum("bhij,bhjv->bhiv", s, vc)
               + w_inter[..., None] * jnp.einsum("bhid,bhdv->bhiv", qc, c_st))
        den = jnp.sum(s, axis=-1) + w_inter * jnp.einsum("bhid,bhd->bhi", qc, n_st)
        h = num / jnp.maximum(jnp.abs(den), jnp.exp(-m_row))[..., None]
        b_end = bcum[..., -1]
        log_to_end = b_end[..., None] - bcum + lic
        m_new = jnp.maximum(b_end + m_st, jnp.max(log_to_end, axis=-1))
        w_src = jnp.exp(log_to_end - m_new[..., None])
        decay = jnp.exp(b_end + m_st - m_new)
        c_new = decay[..., None, None] * c_st + jnp.einsum("bhj,bhjd,bhjv->bhdv", w_src, kc, vc)
        n_new = decay[..., None] * n_st + jnp.einsum("bhj,bhjd->bhd", w_src, kc)
        return (c_new, n_new, m_new), h

    init = (jnp.zeros((b, hh, dqk, dv), jnp.float32),
            jnp.zeros((b, hh, dqk), jnp.float32),
            jnp.full((b, hh), NEG, jnp.float32))
    _, h = lax.scan(step, init, (q, k, v, li, lf))
    return h


def retention_chunkwise(q, k, v, log_gamma):
    _, b, hh, _, dqk = q.shape
    dv = v.shape[-1]
    idx = jnp.arange(CHUNK, dtype=jnp.float32)
    rel = idx[:, None] - idx[None, :]
    decay_intra = jnp.where(rel >= 0, jnp.exp(log_gamma[:, None, None] * jnp.maximum(rel, 0.0)), 0.0)
    q_decay = jnp.exp(log_gamma[:, None] * (idx + 1.0))
    k_decay = jnp.exp(log_gamma[:, None] * (CHUNK - 1.0 - idx))
    chunk_decay = jnp.exp(log_gamma * CHUNK)
    scores = jnp.einsum("nbhid,nbhjd->nbhij", q, k) * decay_intra[None, None]
    intra = jnp.einsum("nbhij,nbhjv->nbhiv", scores, v)
    q_d = q * q_decay[None, None, :, :, None]
    k_d = k * k_decay[None, None, :, :, None]

    def step(state, inp):
        qc, kc, vc = inp
        inter = jnp.einsum("bhid,bhdv->bhiv", qc, state)
        state = chunk_decay[None, :, None, None] * state + jnp.einsum("bhjd,bhjv->bhdv", kc, vc)
        return state, inter

    _, inter = lax.scan(step, jnp.zeros((b, hh, dqk, dv), jnp.float32), (q_d, k_d, v))
    return intra + inter


def token_mixer(u, w_in, ml_conv_w, ml_b_i, ml_b_f, ml_norm_g, rt_norm_g, w_out):
    b, seq_len, _ = u.shape
    n_pad = CHUNK - N_META
    up = jnp.pad(u, ((0, 0), (n_pad, 0), (0, 0)))
    lp = up.shape[1]
    valid = jnp.arange(lp) >= n_pad
    proj = up @ w_in
    offs = [int(o) for o in np.cumsum(COLUMN_SIZES)[:-1]]
    (ml_q, ml_k, ml_v, ml_o, ml_i, ml_f, rt_q, rt_k, rt_v, rt_g) = jnp.split(proj, offs, axis=-1)

    qk = jax.nn.silu(causal_dwconv(jnp.concatenate([ml_q, ml_k], axis=-1), ml_conv_w))
    qk = qk.astype(jnp.float32)
    q = qk[..., : ML_HEADS * ML_DQK].reshape(b, lp, ML_HEADS, ML_DQK)
    k = qk[..., ML_HEADS * ML_DQK:].reshape(b, lp, ML_HEADS, ML_DQK) * (ML_DQK ** -0.5)
    v = ml_v.astype(jnp.float32).reshape(b, lp, ML_HEADS, ML_DV)
    i_pre = ml_i.astype(jnp.float32) + ml_b_i.astype(jnp.float32)
    li = GATE_CAP * jnp.tanh(i_pre / GATE_CAP)
    li = jnp.where(valid[None, :, None], li, NEG)
    lf = jax.nn.log_sigmoid(ml_f.astype(jnp.float32) + ml_b_f.astype(jnp.float32))
    h_ml = mlstm_chunkwise(to_chunks(q), to_chunks(k), to_chunks(v),
                           to_chunks(li[..., None])[..., 0], to_chunks(lf[..., None])[..., 0])
    h_ml = head_norm(from_chunks(h_ml), ml_norm_g).reshape(b, lp, ML_HEADS * ML_DV)
    h_ml = jax.nn.sigmoid(ml_o.astype(jnp.float32)) * h_ml

    pos = jnp.arange(lp, dtype=jnp.float32)
    rq = rotary(rt_q.astype(jnp.float32).reshape(b, lp, RT_HEADS, RT_DQK), pos)
    rk = rotary(rt_k.astype(jnp.float32).reshape(b, lp, RT_HEADS, RT_DQK), pos)
    rk = rk * (RT_DQK ** -0.5) * valid[None, :, None, None]
    rv = rt_v.astype(jnp.float32).reshape(b, lp, RT_HEADS, RT_DV)
    log_gamma = jnp.log1p(-jnp.exp2(-(5.0 + jnp.arange(RT_HEADS, dtype=jnp.float32))))
    h_rt = retention_chunkwise(to_chunks(rq), to_chunks(rk), to_chunks(rv), log_gamma)
    h_rt = head_norm(from_chunks(h_rt), rt_norm_g).reshape(b, lp, RT_HEADS * RT_DV)
    h_rt = jax.nn.silu(rt_g.astype(jnp.float32)) * h_rt

    mixed = jnp.concatenate([h_ml, h_rt], axis=-1).astype(u.dtype)
    return (mixed @ w_out)[:, n_pad:]


def channel_mixer(u, w_up, w_gate, ffn_conv_w, w_down):
    a = causal_dwconv(u @ w_up, ffn_conv_w)
    return (jax.nn.silu(a) * (u @ w_gate)) @ w_down


def setup_inputs(seed: int = 0) -> dict:
    key = jax.random.key(seed)
    ks = jax.random.split(key, 20)
    f32 = jnp.float32
    nrm = lambda k, s, sc: jax.random.normal(k, s, f32) * sc
    return {
        "x": nrm(ks[0], (BATCH, SEQ, D_MODEL), 1.0),
        "meta_tokens": nrm(ks[1], (N_META, D_MODEL), 1.0),
        "norm_mix_g": 1.0 + nrm(ks[2], (DEPTH, D_MODEL), 0.02),
        "w_in": nrm(ks[3], (DEPTH, D_MODEL, D_IN), D_MODEL ** -0.5),
        "ml_conv_w": nrm(ks[4], (DEPTH, ML_CONV, 2 * ML_HEADS * ML_DQK), ML_CONV ** -0.5),
        "ml_b_i": nrm(ks[5], (DEPTH, ML_HEADS), 0.1),
        "ml_b_f": jnp.linspace(3.0, 6.0, ML_HEADS, dtype=f32)[None, :] + nrm(ks[6], (DEPTH, ML_HEADS), 0.1),
        "ml_norm_g": 1.0 + nrm(ks[7], (DEPTH, ML_HEADS * ML_DV), 0.02),
        "rt_norm_g": 1.0 + nrm(ks[8], (DEPTH, RT_HEADS * RT_DV), 0.02),
        "w_out": nrm(ks[9], (DEPTH, D_MIX, D_MODEL), D_MIX ** -0.5),
        "norm_ffn_g": 1.0 + nrm(ks[10], (DEPTH, D_MODEL), 0.02),
        "w_up": nrm(ks[11], (DEPTH, D_MODEL, D_FF), D_MODEL ** -0.5),
        "w_gate": nrm(ks[12], (DEPTH, D_MODEL, D_FF), D_MODEL ** -0.5),
        "ffn_conv_w": nrm(ks[13], (DEPTH, FFN_CONV, D_FF), FFN_CONV ** -0.5),
        "w_down": nrm(ks[14], (DEPTH, D_FF, D_MODEL), D_FF ** -0.5),
        "norm_final_g": 1.0 + nrm(ks[15], (D_MODEL,), 0.02),
    }


def reference(x, meta_tokens, norm_mix_g, w_in, ml_conv_w, ml_b_i, ml_b_f, ml_norm_g, rt_norm_g,
              w_out, norm_ffn_g, w_up, w_gate, ffn_conv_w, w_down, norm_final_g):
    b, _, d = x.shape
    meta = jnp.broadcast_to(meta_tokens.astype(x.dtype)[None], (b, N_META, d))
    h = jnp.concatenate([meta, x], axis=1)
    for l in range(DEPTH):
        h = h + token_mixer(rmsnorm(h, norm_mix_g[l]), w_in[l], ml_conv_w[l], ml_b_i[l], ml_b_f[l],
                            ml_norm_g[l], rt_norm_g[l], w_out[l]).astype(h.dtype)
        h = h + channel_mixer(rmsnorm(h, norm_ffn_g[l]), w_up[l], w_gate[l], ffn_conv_w[l],
                              w_down[l]).astype(h.dtype)
    return rmsnorm(h, norm_final_g)[:, N_META:]
```

```python
import functools
import math

import jax
import jax.numpy as jnp
from jax import lax
from jax.experimental import pallas as pl
from jax.experimental.pallas import tpu as pltpu

F32 = jnp.float32
BF16 = jnp.bfloat16

D_MODEL = 1024
N_META = 16
CHUNK = 64
N_PAD = CHUNK - N_META
EPS = 1e-6
NEG = -1e30
HEADS = 4
DQK = 128
DV = 256
ML_CONV = 4
GATE_CAP = 15.0
ROPE_BASE = 10000.0
D_MIX = 2 * HEADS * DV
D_FF = 2816
FFN_CONV = 3
LANES = 128
SUBLANES = 8

OFF_MQ = 0
OFF_MK = OFF_MQ + HEADS * DQK
OFF_MV = OFF_MK + HEADS * DQK
OFF_MO = OFF_MV + HEADS * DV
OFF_RQ = OFF_MO + HEADS * DV
OFF_RK = OFF_RQ + HEADS * DQK
OFF_RV = OFF_RK + HEADS * DQK
OFF_RG = OFF_RV + HEADS * DV
D_MAIN = OFF_RG + HEADS * DV

TILE_MIX = 256
TILE_FFN = 256
FFN_COLS = 256
VMEM_LIMIT_BYTES = 56 * 1024 * 1024

LOG_GAMMA = tuple(math.log1p(-(2.0 ** -(5.0 + h))) for h in range(HEADS))


def _dot(a, b):
    return jnp.dot(a, b, preferred_element_type=F32)


def _dot_nt(a, b):
    return lax.dot_general(a, b, (((1,), (1,)), ((), ())), preferred_element_type=F32)


def _silu(x):
    return x / (1.0 + jnp.exp(-x))


def _sigmoid(x):
    return 1.0 / (1.0 + jnp.exp(-x))


def _rmsnorm(x, g):
    ms = jnp.mean(x * x, axis=-1, keepdims=True)
    return x * lax.rsqrt(ms + EPS) * g


def _head_norm(h, g):
    mu = jnp.mean(h, axis=-1, keepdims=True)
    c = h - mu
    var = jnp.mean(c * c, axis=-1, keepdims=True)
    return c * lax.rsqrt(var + EPS) * g


def _cumsum_lanes(x):
    n = x.shape[-1]
    lane = lax.broadcasted_iota(jnp.int32, x.shape, x.ndim - 1)
    s = 1
    while s < n:
        x = x + jnp.where(lane >= s, pltpu.roll(x, s, x.ndim - 1), 0.0)
        s *= 2
    return x


def _rope_kernel(pos0, rows, invf_ref, cos_ref, sin_ref):
    base = pos0 + pl.program_id(0) * rows
    pos = (lax.broadcasted_iota(jnp.int32, (rows, LANES), 0) + base).astype(F32)
    lane = lax.broadcasted_iota(jnp.int32, (rows, LANES), 1)
    ang = pos * invf_ref[...]
    cos_ref[...] = jnp.cos(ang)
    s = jnp.sin(ang)
    sin_ref[...] = jnp.where(lane < DQK // 2, -s, s)


def _rope_tables(pos0, n_rows, rows, invf):
    out = jax.ShapeDtypeStruct((n_rows, LANES), F32)
    return pl.pallas_call(
        functools.partial(_rope_kernel, pos0, rows),
        out_shape=(out, out),
        grid=(n_rows // rows,),
        in_specs=[pl.BlockSpec((1, LANES), lambda i: (0, 0))],
        out_specs=(pl.BlockSpec((rows, LANES), lambda i: (i, 0)),
                   pl.BlockSpec((rows, LANES), lambda i: (i, 0))),
        name="rope_tables",
    )(invf)


def _mixer_kernel(T, has_pad, emit_state, *refs):
    (x_ref, gmix_ref, wmain_ref, wg_ref, convw_ref, bias8_ref, mlg_ref, rtg_ref, wout_ref,
     cos_ref, sin_ref, c0_ref, n0_ref, m0_ref, s0_ref, cc0_ref) = refs[:16]
    h1_ref = refs[16]
    if emit_state:
        c_out, n_out, m_out, s_out, cc_out = refs[17:22]
        scratch = refs[22:]
    else:
        scratch = refs[17:]
    c_scr, n_scr, m_scr, s_scr, qk_scr, dmat_scr, qd_scr, kd_scr, mixed_scr = scratch

    b = pl.program_id(0)
    t = pl.program_id(1)
    n_t = pl.num_programs(1)

    @pl.when(t == 0)
    def _load_state():
        c_scr[...] = c0_ref[...]
        n_scr[...] = n0_ref[...]
        m_scr[...] = m0_ref[...]
        s_scr[...] = s0_ref[...]
        qk_scr[0:SUBLANES, :] = cc0_ref[...]

    @pl.when((b == 0) & (t == 0))
    def _build_decay_tables():
        ii = lax.broadcasted_iota(jnp.int32, (T, T), 0)
        jj = lax.broadcasted_iota(jnp.int32, (T, T), 1)
        rel = (ii - jj).astype(F32)
        ri = lax.broadcasted_iota(jnp.int32, (T, LANES), 0).astype(F32)
        for h in range(HEADS):
            lg = LOG_GAMMA[h]
            dmat_scr[h] = jnp.where(rel >= 0.0, jnp.exp(lg * jnp.maximum(rel, 0.0)), 0.0)
            qd_scr[h] = jnp.exp(lg * (ri + 1.0))
            kd_scr[h] = jnp.exp(lg * (T - 1.0 - ri))

    x = x_ref[...]
    u = _rmsnorm(x, gmix_ref[...]).astype(BF16)

    qk_scr[SUBLANES:SUBLANES + T, :] = _dot(u, wmain_ref[:, OFF_MQ:OFF_MV])

    def conv_silu(c0):
        cols = slice(c0, c0 + DQK)
        acc = convw_ref[ML_CONV - 1:ML_CONV, cols] * qk_scr[SUBLANES:SUBLANES + T, cols]
        for k in range(ML_CONV - 1):
            r0 = SUBLANES - (ML_CONV - 1) + k
            acc = acc + convw_ref[k:k + 1, cols] * qk_scr[r0:r0 + T, cols]
        return _silu(acc)

    g_pre = _dot(u, wg_ref[...])
    g8 = g_pre.T[0:SUBLANES, :] + bias8_ref[:, 0:1]
    row8 = lax.broadcasted_iota(jnp.int32, (SUBLANES, T), 0)
    lane8 = lax.broadcasted_iota(jnp.int32, (SUBLANES, T), 1)
    li8 = GATE_CAP * jnp.tanh(g8 / GATE_CAP)
    if has_pad:
        li8 = jnp.where(lane8 >= N_PAD, li8, NEG)
    lf8 = jnp.minimum(g8, 0.0) - jnp.log1p(jnp.exp(-jnp.abs(g8)))
    bc8 = _cumsum_lanes(jnp.where(row8 >= HEADS, lf8, 0.0))
    r8 = jnp.where(row8 < HEADS, li8 - pltpu.roll(bc8, HEADS, 0), bc8)
    col = jnp.concatenate([r8, jnp.zeros((LANES - SUBLANES, T), F32)], axis=0).T
    be8 = jnp.sum(jnp.where(lane8 == T - 1, bc8, 0.0), axis=1, keepdims=True)

    ii = lax.broadcasted_iota(jnp.int32, (T, T), 0)
    jj = lax.broadcasted_iota(jnp.int32, (T, T), 1)
    causal = jj <= ii
    if has_pad:
        valid_col = (lax.broadcasted_iota(jnp.int32, (T, 1), 0) >= N_PAD).astype(F32)

    for h in range(HEADS):
        q = conv_silu(OFF_MQ + h * DQK)
        k = conv_silu(OFF_MK + h * DQK) * (DQK ** -0.5)
        v = _dot(u, wmain_ref[:, OFF_MV + h * DV:OFF_MV + (h + 1) * DV]).astype(BF16)
        qb = q.astype(BF16)
        a_col = col[:, h:h + 1]
        b_col = col[:, HEADS + h:HEADS + h + 1]
        a_row = r8[h:h + 1, :]
        m_h = m_scr[HEADS + h:HEADS + h + 1, 0:1]
        be_h = be8[HEADS + h:HEADS + h + 1, :]

        log_d = jnp.where(causal, b_col + a_row, -jnp.inf)
        log_inter = b_col + m_h
        m_row = jnp.maximum(log_inter, jnp.max(log_d, axis=1, keepdims=True))
        w_intra = jnp.exp(log_d - m_row)
        w_inter = jnp.exp(log_inter - m_row)
        s = _dot_nt(qb, k.astype(BF16)) * w_intra
        n_h = n_scr[h:h + 1, :]
        den = (jnp.sum(s, axis=1, keepdims=True)
               + w_inter * jnp.sum(q * n_h, axis=1, keepdims=True))
        num = _dot(s.astype(BF16), v) + w_inter * _dot(qb, c_scr[h].astype(BF16))
        hh = num / jnp.maximum(jnp.abs(den), jnp.exp(-m_row))

        lte = be_h + a_col
        m_new = jnp.maximum(be_h + m_h, jnp.max(lte, axis=0, keepdims=True))
        w_src = jnp.exp(lte - m_new)
        decay = jnp.exp(be_h + m_h - m_new)
        kw = k * w_src
        c_scr[h] = decay * c_scr[h] + _dot(kw.T.astype(BF16), v)
        n_scr[h:h + 1, :] = decay * n_h + jnp.sum(kw, axis=0, keepdims=True)
        m_scr[HEADS + h:HEADS + h + 1, :] = jnp.broadcast_to(m_new, (1, LANES))

        vs = slice(h * DV, (h + 1) * DV)
        o_gate = _dot(u, wmain_ref[:, OFF_MO + h * DV:OFF_MO + (h + 1) * DV])
        y = _head_norm(hh, mlg_ref[:, vs]) * _sigmoid(o_gate)
        mixed_scr[:, vs] = y.astype(BF16)

    qk_scr[0:SUBLANES, :] = qk_scr[T:T + SUBLANES, :]

    rq_all = _dot(u, wmain_ref[:, OFF_RQ:OFF_RK])
    rk_all = _dot(u, wmain_ref[:, OFF_RK:OFF_RV])
    cos_t = cos_ref[...]
    sin_t = sin_ref[...]
    for h in range(HEADS):
        hs = slice(h * DQK, (h + 1) * DQK)
        tq = rq_all[:, hs]
        tk = rk_all[:, hs]
        rq = tq * cos_t + pltpu.roll(tq, DQK // 2, 1) * sin_t
        rk = (tk * cos_t + pltpu.roll(tk, DQK // 2, 1) * sin_t) * (DQK ** -0.5)
        if has_pad:
            rk = rk * valid_col
        rv = _dot(u, wmain_ref[:, OFF_RV + h * DV:OFF_RV + (h + 1) * DV]).astype(BF16)
        scores = _dot_nt(rq.astype(BF16), rk.astype(BF16)) * dmat_scr[h]
        hr = (_dot(scores.astype(BF16), rv)
              + _dot((rq * qd_scr[h]).astype(BF16), s_scr[h].astype(BF16)))
        s_scr[h] = (math.exp(LOG_GAMMA[h] * T) * s_scr[h]
                    + _dot((rk * kd_scr[h]).T.astype(BF16), rv))
        vs = slice(h * DV, (h + 1) * DV)
        g_gate = _dot(u, wmain_ref[:, OFF_RG + h * DV:OFF_RG + (h + 1) * DV])
        y = _head_norm(hr, rtg_ref[:, vs]) * _silu(g_gate)
        mixed_scr[:, HEADS * DV + h * DV:HEADS * DV + (h + 1) * DV] = y.astype(BF16)

    h1_ref[...] = x + _dot(mixed_scr[...], wout_ref[...])

    if emit_state:
        @pl.when(t == n_t - 1)
        def _store_state():
            c_out[...] = c_scr[...]
            n_out[...] = n_scr[...]
            m_out[...] = m_scr[...]
            s_out[...] = s_scr[...]
            cc_out[...] = qk_scr[0:SUBLANES, :]


def _mixer_call(x3, weights, tables, state, T, has_pad, emit_state):
    B, L, _ = x3.shape
    n_t = L // T
    gmix, wmain, wg, convw, bias8, mlg, rtg, wout = weights
    cos_t, sin_t = tables
    c0, n0, m0, s0, cc0 = state
    const2 = lambda b, t: (0, 0)
    const3 = lambda b, t: (0, 0, 0)
    in_specs = [
        pl.BlockSpec((None, T, D_MODEL), lambda b, t: (b, t, 0)),
        pl.BlockSpec(gmix.shape, const2),
        pl.BlockSpec(wmain.shape, const2),
        pl.BlockSpec(wg.shape, const2),
        pl.BlockSpec(convw.shape, const2),
        pl.BlockSpec(bias8.shape, const2),
        pl.BlockSpec(mlg.shape, const2),
        pl.BlockSpec(rtg.shape, const2),
        pl.BlockSpec(wout.shape, const2),
        pl.BlockSpec((T, LANES), lambda b, t: (t, 0)),
        pl.BlockSpec((T, LANES), lambda b, t: (t, 0)),
        pl.BlockSpec(c0.shape, const3),
        pl.BlockSpec(n0.shape, const2),
        pl.BlockSpec(m0.shape, const2),
        pl.BlockSpec(s0.shape, const3),
        pl.BlockSpec(cc0.shape, const2),
    ]
    out_shape = [jax.ShapeDtypeStruct((B, L, D_MODEL), F32)]
    out_specs = [pl.BlockSpec((None, T, D_MODEL), lambda b, t: (b, t, 0))]
    if emit_state:
        for a, cm in ((c0, const3), (n0, const2), (m0, const2), (s0, const3), (cc0, const2)):
            out_shape.append(jax.ShapeDtypeStruct(a.shape, F32))
            out_specs.append(pl.BlockSpec(a.shape, cm))
    scratch = [
        pltpu.VMEM(c0.shape, F32),
        pltpu.VMEM(n0.shape, F32),
        pltpu.VMEM(m0.shape, F32),
        pltpu.VMEM(s0.shape, F32),
        pltpu.VMEM((T + SUBLANES, 2 * HEADS * DQK), F32),
        pltpu.VMEM((HEADS, T, T), F32),
        pltpu.VMEM((HEADS, T, LANES), F32),
        pltpu.VMEM((HEADS, T, LANES), F32),
        pltpu.VMEM((T, D_MIX), BF16),
    ]
    return pl.pallas_call(
        functools.partial(_mixer_kernel, T, has_pad, emit_state),
        out_shape=tuple(out_shape),
        grid=(B, n_t),
        in_specs=in_specs,
        out_specs=tuple(out_specs),
        scratch_shapes=scratch,
        compiler_params=pltpu.CompilerParams(
            dimension_semantics=("arbitrary", "arbitrary"),
            vmem_limit_bytes=VMEM_LIMIT_BYTES),
        name="mixer_prologue" if emit_state else "mixer",
    )(x3, gmix, wmain, wg, convw, bias8, mlg, rtg, wout, cos_t, sin_t, c0, n0, m0, s0, cc0)


def _ffn_kernel(T, emit_state, *refs):
    (h1_ref, gffn_ref, wup_ref, wgate_ref, convw_ref, wdown_ref, gfin_ref, ac0_ref) = refs[:8]
    out_ref = refs[8]
    if emit_state:
        ac_out = refs[9]
        a_scr, z_scr = refs[10:]
    else:
        a_scr, z_scr = refs[9:]
    t = pl.program_id(1)
    n_t = pl.num_programs(1)

    @pl.when(t == 0)
    def _load_state():
        a_scr[0:SUBLANES, :] = ac0_ref[...]

    x1 = h1_ref[...]
    u = _rmsnorm(x1, gffn_ref[...]).astype(BF16)
    for c in range(D_FF // FFN_COLS):
        cols = slice(c * FFN_COLS, (c + 1) * FFN_COLS)
        a_scr[SUBLANES:SUBLANES + T, cols] = _dot(u, wup_ref[:, cols])
        a = convw_ref[FFN_CONV - 1:FFN_CONV, cols] * a_scr[SUBLANES:SUBLANES + T, cols]
        for k in range(FFN_CONV - 1):
            r0 = SUBLANES - (FFN_CONV - 1) + k
            a = a + convw_ref[k:k + 1, cols] * a_scr[r0:r0 + T, cols]
        gate = _dot(u, wgate_ref[:, cols])
        z_scr[:, cols] = (_silu(a) * gate).astype(BF16)
    a_scr[0:SUBLANES, :] = a_scr[T:T + SUBLANES, :]

    y = x1 + _dot(z_scr[...], wdown_ref[...])
    out_ref[...] = _rmsnorm(y, gfin_ref[...])

    if emit_state:
        @pl.when(t == n_t - 1)
        def _store_state():
            ac_out[...] = a_scr[0:SUBLANES, :]


def _ffn_call(h1, weights, ac0, T, emit_state):
    B, L, _ = h1.shape
    n_t = L // T
    gffn, wup, wgate, convw, wdown, gfin = weights
    const2 = lambda b, t: (0, 0)
    in_specs = [
        pl.BlockSpec((None, T, D_MODEL), lambda b, t: (b, t, 0)),
        pl.BlockSpec(gffn.shape, const2),
        pl.BlockSpec(wup.shape, const2),
        pl.BlockSpec(wgate.shape, const2),
        pl.BlockSpec(convw.shape, const2),
        pl.BlockSpec(wdown.shape, const2),
        pl.BlockSpec(gfin.shape, const2),
        pl.BlockSpec(ac0.shape, const2),
    ]
    out_shape = [jax.ShapeDtypeStruct((B, L, D_MODEL), F32)]
    out_specs = [pl.BlockSpec((None, T, D_MODEL), lambda b, t: (b, t, 0))]
    if emit_state:
        out_shape.append(jax.ShapeDtypeStruct(ac0.shape, F32))
        out_specs.append(pl.BlockSpec(ac0.shape, const2))
    return pl.pallas_call(
        functools.partial(_ffn_kernel, T, emit_state),
        out_shape=tuple(out_shape),
        grid=(B, n_t),
        in_specs=in_specs,
        out_specs=tuple(out_specs),
        scratch_shapes=[pltpu.VMEM((T + SUBLANES, D_FF), F32), pltpu.VMEM((T, D_FF), BF16)],
        compiler_params=pltpu.CompilerParams(
            dimension_semantics=("arbitrary", "arbitrary"),
            vmem_limit_bytes=VMEM_LIMIT_BYTES),
        name="ffn_prologue" if emit_state else "ffn",
    )(h1, gffn, wup, wgate, convw, wdown, gfin, ac0)


def kernel(x, meta_tokens, norm_mix_g, w_in, ml_conv_w, ml_b_i, ml_b_f, ml_norm_g, rt_norm_g, w_out,
           norm_ffn_g, w_up, w_gate, ffn_conv_w, w_down, norm_final_g):
    B, L, D = x.shape
    assert D == D_MODEL and L % TILE_MIX == 0 and L % TILE_FFN == 0
    assert w_in.shape[0] == 1, "single-layer block"

    w = w_in[0]
    sizes = (HEADS * DQK, HEADS * DQK, HEADS * DV, HEADS * DV, HEADS, HEADS,
             HEADS * DQK, HEADS * DQK, HEADS * DV, HEADS * DV)
    offs = [0]
    for s in sizes:
        offs.append(offs[-1] + s)
    parts = [w[:, offs[i]:offs[i + 1]] for i in range(len(sizes))]
    wmain = jnp.concatenate(parts[0:4] + parts[6:10], axis=1).astype(BF16)
    wg = jnp.concatenate([parts[4], parts[5],
                          jnp.zeros((D, LANES - 2 * HEADS), F32)], axis=1).astype(BF16)
    bias8 = jnp.broadcast_to(jnp.concatenate([ml_b_i[0], ml_b_f[0]])[:, None], (2 * HEADS, LANES))
    mixer_w = (norm_mix_g[0][None, :], wmain, wg, ml_conv_w[0], bias8,
               ml_norm_g[0][None, :], rt_norm_g[0][None, :], w_out[0].astype(BF16))
    ffn_w = (norm_ffn_g[0][None, :], w_up[0].astype(BF16), w_gate[0].astype(BF16), ffn_conv_w[0],
             w_down[0].astype(BF16), norm_final_g[None, :])

    invf = ROPE_BASE ** (-jnp.arange(0, DQK, 2, dtype=F32) / DQK)
    invf = jnp.concatenate([invf, invf])[None, :]
    tab_meta = _rope_tables(0, CHUNK, CHUNK, invf)
    tab_main = _rope_tables(CHUNK, L, 512, invf)

    meta_chunk = jnp.concatenate([jnp.zeros((N_PAD, D), F32), meta_tokens.astype(F32)], axis=0)[None]
    zero_state = (jnp.zeros((HEADS, DQK, DV), F32), jnp.zeros((SUBLANES, LANES), F32),
                  jnp.full((SUBLANES, LANES), NEG, F32), jnp.zeros((HEADS, DQK, DV), F32),
                  jnp.zeros((SUBLANES, 2 * HEADS * DQK), F32))
    h1_meta, *state = _mixer_call(meta_chunk, mixer_w, tab_meta, zero_state, CHUNK, True, True)
    _, ac0 = _ffn_call(h1_meta, ffn_w, jnp.zeros((SUBLANES, D_FF), F32), CHUNK, True)

    (h1,) = _mixer_call(x, mixer_w, tab_main, tuple(state), TILE_MIX, False, False)
    (out,) = _ffn_call(h1, ffn_w, ac0, TILE_FFN, False)
    return out
```

```python
import functools
import math

import jax
import jax.numpy as jnp
from jax import lax
from jax.experimental import pallas as pl
from jax.experimental.pallas import tpu as pltpu

F32 = jnp.float32
BF16 = jnp.bfloat16

D_MODEL = 1024
N_META = 16
CHUNK = 64
N_PAD = CHUNK - N_META
EPS = 1e-6
NEG = -1e30
HEADS = 4
DQK = 128
DV = 256
ML_CONV = 4
GATE_CAP = 15.0
ROPE_BASE = 10000.0
D_MIX = 2 * HEADS * DV
D_FF = 2816
FFN_CONV = 3
LANES = 128
SUBLANES = 8

OFF_MQ = 0
OFF_MK = OFF_MQ + HEADS * DQK
OFF_MV = OFF_MK + HEADS * DQK
OFF_MO = OFF_MV + HEADS * DV
OFF_RQ = OFF_MO + HEADS * DV
OFF_RK = OFF_RQ + HEADS * DQK
OFF_RV = OFF_RK + HEADS * DQK
OFF_RG = OFF_RV + HEADS * DV
D_MAIN = OFF_RG + HEADS * DV

TILE_MIX = 256
TILE_FFN = 256
BATCH_GROUP = 2
FFN_COLS = 256
VMEM_LIMIT_BYTES = 56 * 1024 * 1024

LOG_GAMMA = tuple(math.log1p(-(2.0 ** -(5.0 + h))) for h in range(HEADS))


def _dot(a, b):
    return jnp.dot(a, b, preferred_element_type=F32)


def _dot_nt(a, b):
    return lax.dot_general(a, b, (((1,), (1,)), ((), ())), preferred_element_type=F32)


def _sigmoid(x):
    return 0.5 + 0.5 * jnp.tanh(0.5 * x)


def _silu(x):
    hx = 0.5 * x
    return hx + hx * jnp.tanh(hx)


def _rmsnorm(x, g):
    ms = jnp.mean(x * x, axis=-1, keepdims=True)
    return x * lax.rsqrt(ms + EPS) * g


def _head_norm(h, g):
    mu = jnp.mean(h, axis=-1, keepdims=True)
    c = h - mu
    var = jnp.mean(c * c, axis=-1, keepdims=True)
    return c * lax.rsqrt(var + EPS) * g


def _cumsum_lanes_mxu(x):
    r, n = x.shape
    hi = x.astype(BF16).astype(F32)
    r1 = x - hi
    mid = r1.astype(BF16).astype(F32)
    lo = (r1 - mid).astype(BF16).astype(F32)
    jsrc = lax.broadcasted_iota(jnp.int32, (n, n), 0)
    jdst = lax.broadcasted_iota(jnp.int32, (n, n), 1)
    tri = jnp.where(jsrc <= jdst, 1.0, 0.0).astype(BF16)
    p = _dot(jnp.concatenate([hi, mid, lo], axis=0).astype(BF16), tri)
    return p[0:r] + p[r:2 * r] + p[2 * r:3 * r]


def _rope_kernel(pos0, rows, invf_ref, cos_ref, sin_ref):
    base = pos0 + pl.program_id(0) * rows
    pos = (lax.broadcasted_iota(jnp.int32, (rows, LANES), 0) + base).astype(F32)
    lane = lax.broadcasted_iota(jnp.int32, (rows, LANES), 1)
    ang = pos * invf_ref[...]
    cos_ref[...] = jnp.cos(ang)
    s = jnp.sin(ang)
    sin_ref[...] = jnp.where(lane < DQK // 2, -s, s)


def _rope_tables(pos0, n_rows, rows, invf):
    out = jax.ShapeDtypeStruct((n_rows, LANES), F32)
    return pl.pallas_call(
        functools.partial(_rope_kernel, pos0, rows),
        out_shape=(out, out),
        grid=(n_rows // rows,),
        in_specs=[pl.BlockSpec((1, LANES), lambda i: (0, 0))],
        out_specs=(pl.BlockSpec((rows, LANES), lambda i: (i, 0)),
                   pl.BlockSpec((rows, LANES), lambda i: (i, 0))),
        name="rope_tables",
    )(invf)


def _mixer_tile(T, has_pad, x_ref, h1_ref, w_refs, tab_refs, state_refs, qk_scr, mixed_scr):
    gmix_ref, wmain_ref, wg_ref, convw_ref, bias8_ref, mlg_ref, rtg_ref, wout_ref = w_refs
    cos_ref, sin_ref, dmat_scr, qd_scr, kd_scr = tab_refs
    c_scr, n_scr, m_scr, s_scr = state_refs

    x = x_ref[...]
    u = _rmsnorm(x, gmix_ref[...]).astype(BF16)

    qk_scr[SUBLANES:SUBLANES + T, :] = _dot(u, wmain_ref[:, OFF_MQ:OFF_MV])

    def conv_silu(c0):
        cols = slice(c0, c0 + DQK)
        acc = convw_ref[ML_CONV - 1:ML_CONV, cols] * qk_scr[SUBLANES:SUBLANES + T, cols]
        for k in range(ML_CONV - 1):
            r0 = SUBLANES - (ML_CONV - 1) + k
            acc = acc + convw_ref[k:k + 1, cols] * qk_scr[r0:r0 + T, cols]
        return _silu(acc)

    g_pre = _dot(u, wg_ref[...])
    g8 = g_pre.T[0:SUBLANES, :] + bias8_ref[:, 0:1]
    row8 = lax.broadcasted_iota(jnp.int32, (SUBLANES, T), 0)
    lane8 = lax.broadcasted_iota(jnp.int32, (SUBLANES, T), 1)
    li8 = GATE_CAP * jnp.tanh(g8 / GATE_CAP)
    if has_pad:
        li8 = jnp.where(lane8 >= N_PAD, li8, NEG)
    lf8 = jnp.minimum(g8, 0.0) - jnp.log1p(jnp.exp(-jnp.abs(g8)))
    bc8 = _cumsum_lanes_mxu(jnp.where(row8 >= HEADS, lf8, 0.0))
    r8 = jnp.where(row8 < HEADS, li8 - pltpu.roll(bc8, HEADS, 0), bc8)
    col = jnp.concatenate([r8, jnp.zeros((LANES - SUBLANES, T), F32)], axis=0).T
    be8 = jnp.sum(jnp.where(lane8 == T - 1, bc8, 0.0), axis=1, keepdims=True)

    ii = lax.broadcasted_iota(jnp.int32, (T, T), 0)
    jj = lax.broadcasted_iota(jnp.int32, (T, T), 1)
    causal = jj <= ii
    if has_pad:
        valid_col = (lax.broadcasted_iota(jnp.int32, (T, 1), 0) >= N_PAD).astype(F32)

    rq_all = _dot(u, wmain_ref[:, OFF_RQ:OFF_RK])
    rk_all = _dot(u, wmain_ref[:, OFF_RK:OFF_RV])
    cos_t = cos_ref[...]
    sin_t = sin_ref[...]
    for h in range(HEADS):
        hs = slice(h * DQK, (h + 1) * DQK)
        tq = rq_all[:, hs]
        tk = rk_all[:, hs]
        rq = tq * cos_t + pltpu.roll(tq, DQK // 2, 1) * sin_t
        rk = (tk * cos_t + pltpu.roll(tk, DQK // 2, 1) * sin_t) * (DQK ** -0.5)
        if has_pad:
            rk = rk * valid_col
        rv = _dot(u, wmain_ref[:, OFF_RV + h * DV:OFF_RV + (h + 1) * DV]).astype(BF16)
        scores = _dot_nt(rq.astype(BF16), rk.astype(BF16)) * dmat_scr[h]
        hr = (_dot(scores.astype(BF16), rv)
              + _dot((rq * qd_scr[h]).astype(BF16), s_scr[h].astype(BF16)))
        s_scr[h] = (math.exp(LOG_GAMMA[h] * T) * s_scr[h]
                    + _dot((rk * kd_scr[h]).T.astype(BF16), rv))
        vs = slice(h * DV, (h + 1) * DV)
        g_gate = _dot(u, wmain_ref[:, OFF_RG + h * DV:OFF_RG + (h + 1) * DV])
        y = _head_norm(hr, rtg_ref[:, vs]) * _silu(g_gate)
        mixed_scr[:, HEADS * DV + h * DV:HEADS * DV + (h + 1) * DV] = y.astype(BF16)

    for h in range(HEADS):
        q = conv_silu(OFF_MQ + h * DQK)
        k = conv_silu(OFF_MK + h * DQK) * (DQK ** -0.5)
        v = _dot(u, wmain_ref[:, OFF_MV + h * DV:OFF_MV + (h + 1) * DV]).astype(BF16)
        qb = q.astype(BF16)
        a_col = col[:, h:h + 1]
        b_col = col[:, HEADS + h:HEADS + h + 1]
        a_row = r8[h:h + 1, :]
        m_h = m_scr[HEADS + h:HEADS + h + 1, 0:1]
        be_h = be8[HEADS + h:HEADS + h + 1, :]

        log_d = jnp.where(causal, b_col + a_row, -jnp.inf)
        log_inter = b_col + m_h
        m_row = jnp.maximum(log_inter, jnp.max(log_d, axis=1, keepdims=True))
        w_intra = jnp.exp(log_d - m_row)
        w_inter = jnp.exp(log_inter - m_row)
        s = _dot_nt(qb, k.astype(BF16)) * w_intra
        n_h = n_scr[h:h + 1, :]
        den = (jnp.sum(s, axis=1, keepdims=True)
               + w_inter * jnp.sum(q * n_h, axis=1, keepdims=True))
        num = _dot(s.astype(BF16), v) + w_inter * _dot(qb, c_scr[h].astype(BF16))
        hh = num / jnp.maximum(jnp.abs(den), jnp.exp(-m_row))

        lte = be_h + a_col
        m_new = jnp.maximum(be_h + m_h, jnp.max(lte, axis=0, keepdims=True))
        w_src = jnp.exp(lte - m_new)
        decay = jnp.exp(be_h + m_h - m_new)
        kw = k * w_src
        c_scr[h] = decay * c_scr[h] + _dot(kw.T.astype(BF16), v)
        n_scr[h:h + 1, :] = decay * n_h + jnp.sum(kw, axis=0, keepdims=True)
        m_scr[HEADS + h:HEADS + h + 1, :] = jnp.broadcast_to(m_new, (1, LANES))

        vs = slice(h * DV, (h + 1) * DV)
        o_gate = _dot(u, wmain_ref[:, OFF_MO + h * DV:OFF_MO + (h + 1) * DV])
        y = _head_norm(hh, mlg_ref[:, vs]) * _sigmoid(o_gate)
        mixed_scr[:, vs] = y.astype(BF16)

    qk_scr[0:SUBLANES, :] = qk_scr[T:T + SUBLANES, :]

    h1_ref[...] = x + _dot(mixed_scr[...], wout_ref[...])


def _mixer_kernel(T, G, has_pad, emit_state, *refs):
    x_ref = refs[0]
    w_refs = refs[1:9]
    cos_ref, sin_ref, c0_ref, n0_ref, m0_ref, s0_ref, cc0_ref = refs[9:16]
    h1_ref = refs[16]
    if emit_state:
        c_out, n_out, m_out, s_out, cc_out = refs[17:22]
        scratch = refs[22:]
    else:
        scratch = refs[17:]
    c_scr, n_scr, m_scr, s_scr, qk_scr, dmat_scr, qd_scr, kd_scr, mixed_scr = scratch

    b = pl.program_id(0)
    t = pl.program_id(1)
    n_t = pl.num_programs(1)

    @pl.when(t == 0)
    def _load_state():
        for g in range(G):
            c_scr[g] = c0_ref[...]
            n_scr[g] = n0_ref[...]
            m_scr[g] = m0_ref[...]
            s_scr[g] = s0_ref[...]
            qk_scr[g, 0:SUBLANES, :] = cc0_ref[...]

    @pl.when((b == 0) & (t == 0))
    def _build_decay_tables():
        ii = lax.broadcasted_iota(jnp.int32, (T, T), 0)
        jj = lax.broadcasted_iota(jnp.int32, (T, T), 1)
        rel = (ii - jj).astype(F32)
        ri = lax.broadcasted_iota(jnp.int32, (T, LANES), 0).astype(F32)
        for h in range(HEADS):
            lg = LOG_GAMMA[h]
            dmat_scr[h] = jnp.where(rel >= 0.0, jnp.exp(lg * jnp.maximum(rel, 0.0)), 0.0)
            qd_scr[h] = jnp.exp(lg * (ri + 1.0))
            kd_scr[h] = jnp.exp(lg * (T - 1.0 - ri))

    for g in range(G):
        _mixer_tile(T, has_pad, x_ref.at[g], h1_ref.at[g], w_refs,
                    (cos_ref, sin_ref, dmat_scr, qd_scr, kd_scr),
                    (c_scr.at[g], n_scr.at[g], m_scr.at[g], s_scr.at[g]),
                    qk_scr.at[g], mixed_scr.at[g])

    if emit_state:
        @pl.when(t == n_t - 1)
        def _store_state():
            c_out[...] = c_scr[0]
            n_out[...] = n_scr[0]
            m_out[...] = m_scr[0]
            s_out[...] = s_scr[0]
            cc_out[...] = qk_scr[0, 0:SUBLANES, :]


def _mixer_call(x3, weights, tables, state, T, G, has_pad, emit_state):
    B, L, _ = x3.shape
    n_t = L // T
    gmix, wmain, wg, convw, bias8, mlg, rtg, wout = weights
    cos_t, sin_t = tables
    c0, n0, m0, s0, cc0 = state
    const2 = lambda b, t: (0, 0)
    const3 = lambda b, t: (0, 0, 0)
    in_specs = [
        pl.BlockSpec((G, T, D_MODEL), lambda b, t: (b, t, 0)),
        pl.BlockSpec(gmix.shape, const2),
        pl.BlockSpec(wmain.shape, const2),
        pl.BlockSpec(wg.shape, const2),
        pl.BlockSpec(convw.shape, const2),
        pl.BlockSpec(bias8.shape, const2),
        pl.BlockSpec(mlg.shape, const2),
        pl.BlockSpec(rtg.shape, const2),
        pl.BlockSpec(wout.shape, const2),
        pl.BlockSpec((T, LANES), lambda b, t: (t, 0)),
        pl.BlockSpec((T, LANES), lambda b, t: (t, 0)),
        pl.BlockSpec(c0.shape, const3),
        pl.BlockSpec(n0.shape, const2),
        pl.BlockSpec(m0.shape, const2),
        pl.BlockSpec(s0.shape, const3),
        pl.BlockSpec(cc0.shape, const2),
    ]
    out_shape = [jax.ShapeDtypeStruct((B, L, D_MODEL), F32)]
    out_specs = [pl.BlockSpec((G, T, D_MODEL), lambda b, t: (b, t, 0))]
    if emit_state:
        for a, cm in ((c0, const3), (n0, const2), (m0, const2), (s0, const3), (cc0, const2)):
            out_shape.append(jax.ShapeDtypeStruct(a.shape, F32))
            out_specs.append(pl.BlockSpec(a.shape, cm))
    scratch = [
        pltpu.VMEM((G,) + c0.shape, F32),
        pltpu.VMEM((G,) + n0.shape, F32),
        pltpu.VMEM((G,) + m0.shape, F32),
        pltpu.VMEM((G,) + s0.shape, F32),
        pltpu.VMEM((G, T + SUBLANES, 2 * HEADS * DQK), F32),
        pltpu.VMEM((HEADS, T, T), F32),
        pltpu.VMEM((HEADS, T, LANES), F32),
        pltpu.VMEM((HEADS, T, LANES), F32),
        pltpu.VMEM((G, T, D_MIX), BF16),
    ]
    return pl.pallas_call(
        functools.partial(_mixer_kernel, T, G, has_pad, emit_state),
        out_shape=tuple(out_shape),
        grid=(B // G, n_t),
        in_specs=in_specs,
        out_specs=tuple(out_specs),
        scratch_shapes=scratch,
        compiler_params=pltpu.CompilerParams(
            dimension_semantics=("arbitrary", "arbitrary"),
            vmem_limit_bytes=VMEM_LIMIT_BYTES),
        name="mixer_prologue" if emit_state else "mixer",
    )(x3, gmix, wmain, wg, convw, bias8, mlg, rtg, wout, cos_t, sin_t, c0, n0, m0, s0, cc0)


def _ffn_tile(T, h1_ref, out_ref, w_refs, a_scr, z_scr):
    gffn_ref, wup_ref, wgate_ref, convw_ref, wdown_ref, gfin_ref = w_refs
    x1 = h1_ref[...]
    u = _rmsnorm(x1, gffn_ref[...]).astype(BF16)
    for c in range(D_FF // FFN_COLS):
        cols = slice(c * FFN_COLS, (c + 1) * FFN_COLS)
        a_scr[SUBLANES:SUBLANES + T, cols] = _dot(u, wup_ref[:, cols])
        a = convw_ref[FFN_CONV - 1:FFN_CONV, cols] * a_scr[SUBLANES:SUBLANES + T, cols]
        for k in range(FFN_CONV - 1):
            r0 = SUBLANES - (FFN_CONV - 1) + k
            a = a + convw_ref[k:k + 1, cols] * a_scr[r0:r0 + T, cols]
        gate = _dot(u, wgate_ref[:, cols])
        z_scr[:, cols] = (_silu(a) * gate).astype(BF16)
    a_scr[0:SUBLANES, :] = a_scr[T:T + SUBLANES, :]

    y = x1 + _dot(z_scr[...], wdown_ref[...])
    out_ref[...] = _rmsnorm(y, gfin_ref[...])


def _ffn_kernel(T, G, emit_state, *refs):
    h1_ref = refs[0]
    w_refs = refs[1:7]
    ac0_ref = refs[7]
    out_ref = refs[8]
    if emit_state:
        ac_out = refs[9]
        a_scr, z_scr = refs[10:]
    else:
        a_scr, z_scr = refs[9:]
    t = pl.program_id(1)
    n_t = pl.num_programs(1)

    @pl.when(t == 0)
    def _load_state():
        for g in range(G):
            a_scr[g, 0:SUBLANES, :] = ac0_ref[...]

    for g in range(G):
        _ffn_tile(T, h1_ref.at[g], out_ref.at[g], w_refs, a_scr.at[g], z_scr.at[g])

    if emit_state:
        @pl.when(t == n_t - 1)
        def _store_state():
            ac_out[...] = a_scr[0, 0:SUBLANES, :]


def _ffn_call(h1, weights, ac0, T, G, emit_state):
    B, L, _ = h1.shape
    n_t = L // T
    gffn, wup, wgate, convw, wdown, gfin = weights
    const2 = lambda b, t: (0, 0)
    in_specs = [
        pl.BlockSpec((G, T, D_MODEL), lambda b, t: (b, t, 0)),
        pl.BlockSpec(gffn.shape, const2),
        pl.BlockSpec(wup.shape, const2),
        pl.BlockSpec(wgate.shape, const2),
        pl.BlockSpec(convw.shape, const2),
        pl.BlockSpec(wdown.shape, const2),
        pl.BlockSpec(gfin.shape, const2),
        pl.BlockSpec(ac0.shape, const2),
    ]
    out_shape = [jax.ShapeDtypeStruct((B, L, D_MODEL), F32)]
    out_specs = [pl.BlockSpec((G, T, D_MODEL), lambda b, t: (b, t, 0))]
    if emit_state:
        out_shape.append(jax.ShapeDtypeStruct(ac0.shape, F32))
        out_specs.append(pl.BlockSpec(ac0.shape, const2))
    return pl.pallas_call(
        functools.partial(_ffn_kernel, T, G, emit_state),
        out_shape=tuple(out_shape),
        grid=(B // G, n_t),
        in_specs=in_specs,
        out_specs=tuple(out_specs),
        scratch_shapes=[pltpu.VMEM((G, T + SUBLANES, D_FF), F32), pltpu.VMEM((G, T, D_FF), BF16)],
        compiler_params=pltpu.CompilerParams(
            dimension_semantics=("arbitrary", "arbitrary"),
            vmem_limit_bytes=VMEM_LIMIT_BYTES),
        name="ffn_prologue" if emit_state else "ffn",
    )(h1, gffn, wup, wgate, convw, wdown, gfin, ac0)


def kernel(x, meta_tokens, norm_mix_g, w_in, ml_conv_w, ml_b_i, ml_b_f, ml_norm_g, rt_norm_g, w_out,
           norm_ffn_g, w_up, w_gate, ffn_conv_w, w_down, norm_final_g):
    B, L, D = x.shape
    assert D == D_MODEL and L % TILE_MIX == 0 and L % TILE_FFN == 0
    assert w_in.shape[0] == 1, "single-layer block"

    w = w_in[0]
    sizes = (HEADS * DQK, HEADS * DQK, HEADS * DV, HEADS * DV, HEADS, HEADS,
             HEADS * DQK, HEADS * DQK, HEADS * DV, HEADS * DV)
    offs = [0]
    for s in sizes:
        offs.append(offs[-1] + s)
    parts = [w[:, offs[i]:offs[i + 1]] for i in range(len(sizes))]
    wmain = jnp.concatenate(parts[0:4] + parts[6:10], axis=1).astype(BF16)
    wg = jnp.concatenate([parts[4], parts[5],
                          jnp.zeros((D, LANES - 2 * HEADS), F32)], axis=1).astype(BF16)
    bias8 = jnp.broadcast_to(jnp.concatenate([ml_b_i[0], ml_b_f[0]])[:, None], (2 * HEADS, LANES))
    mixer_w = (norm_mix_g[0][None, :], wmain, wg, ml_conv_w[0], bias8,
               ml_norm_g[0][None, :], rt_norm_g[0][None, :], w_out[0].astype(BF16))
    ffn_w = (norm_ffn_g[0][None, :], w_up[0].astype(BF16), w_gate[0].astype(BF16), ffn_conv_w[0],
             w_down[0].astype(BF16), norm_final_g[None, :])

    invf = ROPE_BASE ** (-jnp.arange(0, DQK, 2, dtype=F32) / DQK)
    invf = jnp.concatenate([invf, invf])[None, :]
    tab_meta = _rope_tables(0, CHUNK, CHUNK, invf)
    tab_main = _rope_tables(CHUNK, L, 512, invf)

    meta_chunk = jnp.concatenate([jnp.zeros((N_PAD, D), F32), meta_tokens.astype(F32)], axis=0)[None]
    zero_state = (jnp.zeros((HEADS, DQK, DV), F32), jnp.zeros((SUBLANES, LANES), F32),
                  jnp.full((SUBLANES, LANES), NEG, F32), jnp.zeros((HEADS, DQK, DV), F32),
                  jnp.zeros((SUBLANES, 2 * HEADS * DQK), F32))
    h1_meta, *state = _mixer_call(meta_chunk, mixer_w, tab_meta, zero_state, CHUNK, 1, True, True)
    _, ac0 = _ffn_call(h1_meta, ffn_w, jnp.zeros((SUBLANES, D_FF), F32), CHUNK, 1, True)

    G = BATCH_GROUP if B % BATCH_GROUP == 0 else 1
    (h1,) = _mixer_call(x, mixer_w, tab_main, tuple(state), TILE_MIX, G, False, False)
    (out,) = _ffn_call(h1, ffn_w, ac0, TILE_FFN, G, False)
    return out
```

```python
import functools
import math

import jax
import jax.numpy as jnp
from jax import lax
from jax.experimental import pallas as pl
from jax.experimental.pallas import tpu as pltpu

F32 = jnp.float32
BF16 = jnp.bfloat16

D_MODEL = 1024
N_META = 16
CHUNK = 64
N_PAD = CHUNK - N_META
EPS = 1e-6
NEG = -1e30
HEADS = 4
DQK = 128
DV = 256
ML_CONV = 4
GATE_CAP = 15.0
ROPE_BASE = 10000.0
D_MIX = 2 * HEADS * DV
D_FF = 2816
FFN_CONV = 3
LANES = 128
SUBLANES = 8
BF16_SUBLANES = 16

OFF_Q = 0
OFF_K = OFF_Q + HEADS * DQK
OFF_V = OFF_K + HEADS * DQK
OFF_G = OFF_V + HEADS * DV
D_GROUP = OFF_G + HEADS * DV

TILE_MIX = 256
TILE_FFN = 256
BATCH_GROUP = 2
FFN_COLS = 256
VMEM_LIMIT_BYTES = 56 * 1024 * 1024

LOG_GAMMA = tuple(math.log1p(-(2.0 ** -(5.0 + h))) for h in range(HEADS))


def _dot(a, b):
    return jnp.dot(a, b, preferred_element_type=F32)


def _dot_nt(a, b):
    return lax.dot_general(a, b, (((1,), (1,)), ((), ())), preferred_element_type=F32)


def _sigmoid(x):
    return 0.5 + 0.5 * jnp.tanh(0.5 * x)


def _silu(x):
    hx = 0.5 * x
    return hx + hx * jnp.tanh(hx)


def _rmsnorm(x, g):
    ms = jnp.mean(x * x, axis=-1, keepdims=True)
    return x * lax.rsqrt(ms + EPS) * g


def _head_norm(h, g):
    mu = jnp.mean(h, axis=-1, keepdims=True)
    c = h - mu
    var = jnp.mean(c * c, axis=-1, keepdims=True)
    return c * lax.rsqrt(var + EPS) * g


def _cumsum_lanes_mxu(x):
    r, n = x.shape
    hi = x.astype(BF16).astype(F32)
    r1 = x - hi
    mid = r1.astype(BF16).astype(F32)
    lo = (r1 - mid).astype(BF16).astype(F32)
    jsrc = lax.broadcasted_iota(jnp.int32, (n, n), 0)
    jdst = lax.broadcasted_iota(jnp.int32, (n, n), 1)
    tri = jnp.where(jsrc <= jdst, 1.0, 0.0).astype(BF16)
    p = _dot(jnp.concatenate([hi, mid, lo], axis=0).astype(BF16), tri)
    return p[0:r] + p[r:2 * r] + p[2 * r:3 * r]


def _trace_alternately(stage_generators):
    live = list(stage_generators)
    while live:
        still = []
        for gen in live:
            try:
                next(gen)
                still.append(gen)
            except StopIteration:
                pass
        live = still


def _rope_kernel(pos0, n_blocks, invf_ref, cos_ref, sin_ref):
    invf = invf_ref[...]
    lane = lax.broadcasted_iota(jnp.int32, (CHUNK, LANES), 1)
    off = lax.broadcasted_iota(jnp.int32, (CHUNK, LANES), 0).astype(F32) * invf
    cos_o, sin_o = jnp.cos(off), jnp.sin(off)
    base = ((lax.broadcasted_iota(jnp.int32, (n_blocks, LANES), 0) * CHUNK + pos0).astype(F32) * invf)
    cos_b, sin_b = jnp.cos(base), jnp.sin(base)
    for a in range(n_blocks):
        cb = cos_b[a:a + 1, :]
        sb = sin_b[a:a + 1, :]
        rows = slice(a * CHUNK, (a + 1) * CHUNK)
        cos_ref[rows, :] = cb * cos_o - sb * sin_o
        s = sb * cos_o + cb * sin_o
        sin_ref[rows, :] = jnp.where(lane < DQK // 2, -s, s)


def _rope_tables(pos0, n_rows, invf):
    assert pos0 % CHUNK == 0 and n_rows % CHUNK == 0
    out = jax.ShapeDtypeStruct((n_rows, LANES), F32)
    return pl.pallas_call(
        functools.partial(_rope_kernel, pos0, n_rows // CHUNK),
        out_shape=(out, out),
        name="rope_tables",
    )(invf)


def _mixer_tile(T, has_pad, x_ref, h1_ref, w_refs, tab_refs, state_refs, qk_scr, mixed_scr):
    gmix_ref, wml_ref, wrt_ref, wg_ref, convw_ref, bias8_ref, mlg_ref, rtg_ref, wout_ref = w_refs
    cos_ref, sin_ref, dmat_scr, qd_scr, kd_scr = tab_refs
    c_scr, n_scr, m_scr, s_scr = state_refs

    x = x_ref[...]
    u = _rmsnorm(x, gmix_ref[...]).astype(BF16)

    qk_scr[SUBLANES:SUBLANES + T, :] = _dot(u, wml_ref[:, OFF_Q:OFF_V])

    def conv_silu(c0):
        cols = slice(c0, c0 + DQK)
        acc = convw_ref[ML_CONV - 1:ML_CONV, cols] * qk_scr[SUBLANES:SUBLANES + T, cols]
        for k in range(ML_CONV - 1):
            r0 = SUBLANES - (ML_CONV - 1) + k
            acc = acc + convw_ref[k:k + 1, cols] * qk_scr[r0:r0 + T, cols]
        return _silu(acc)

    g_pre = _dot(u, wg_ref[...])
    g8 = g_pre.T[0:SUBLANES, :] + bias8_ref[:, 0:1]
    row8 = lax.broadcasted_iota(jnp.int32, (SUBLANES, T), 0)
    lane8 = lax.broadcasted_iota(jnp.int32, (SUBLANES, T), 1)
    li8 = GATE_CAP * jnp.tanh(g8 / GATE_CAP)
    if has_pad:
        li8 = jnp.where(lane8 >= N_PAD, li8, NEG)
    lf8 = jnp.minimum(g8, 0.0) - jnp.log1p(jnp.exp(-jnp.abs(g8)))
    bc8 = _cumsum_lanes_mxu(jnp.where(row8 >= HEADS, lf8, 0.0))
    r8 = jnp.where(row8 < HEADS, li8 - pltpu.roll(bc8, HEADS, 0), bc8)
    col = jnp.concatenate([r8, jnp.zeros((LANES - SUBLANES, T), F32)], axis=0).T
    be8 = jnp.sum(jnp.where(lane8 == T - 1, bc8, 0.0), axis=1, keepdims=True)

    ii = lax.broadcasted_iota(jnp.int32, (T, T), 0)
    jj = lax.broadcasted_iota(jnp.int32, (T, T), 1)
    causal = jj <= ii
    if has_pad:
        valid_col = (lax.broadcasted_iota(jnp.int32, (T, 1), 0) >= N_PAD).astype(F32)

    yield
    rq_all = _dot(u, wrt_ref[:, OFF_Q:OFF_K])
    rk_all = _dot(u, wrt_ref[:, OFF_K:OFF_V])
    cos_t = cos_ref[...]
    sin_t = sin_ref[...]
    for h in range(HEADS):
        hs = slice(h * DQK, (h + 1) * DQK)
        tq = rq_all[:, hs]
        tk = rk_all[:, hs]
        rq = tq * cos_t + pltpu.roll(tq, DQK // 2, 1) * sin_t
        rk = (tk * cos_t + pltpu.roll(tk, DQK // 2, 1) * sin_t) * (DQK ** -0.5)
        if has_pad:
            rk = rk * valid_col
        rv = _dot(u, wrt_ref[:, OFF_V + h * DV:OFF_V + (h + 1) * DV]).astype(BF16)
        yield
        scores = _dot_nt(rq.astype(BF16), rk.astype(BF16)) * dmat_scr[h]
        hr = (_dot(scores.astype(BF16), rv)
              + _dot((rq * qd_scr[h]).astype(BF16), s_scr[h].astype(BF16)))
        s_scr[h] = (math.exp(LOG_GAMMA[h] * T) * s_scr[h]
                    + _dot((rk * kd_scr[h]).T.astype(BF16), rv))
        yield
        vs = slice(h * DV, (h + 1) * DV)
        g_gate = _dot(u, wrt_ref[:, OFF_G + h * DV:OFF_G + (h + 1) * DV])
        y = _head_norm(hr, rtg_ref[:, vs]) * _silu(g_gate)
        mixed_scr[:, HEADS * DV + h * DV:HEADS * DV + (h + 1) * DV] = y.astype(BF16)
        yield

    for h in range(HEADS):
        q = conv_silu(OFF_Q + h * DQK)
        k = conv_silu(OFF_K + h * DQK) * (DQK ** -0.5)
        v = _dot(u, wml_ref[:, OFF_V + h * DV:OFF_V + (h + 1) * DV]).astype(BF16)
        qb = q.astype(BF16)
        yield
        a_col = col[:, h:h + 1]
        b_col = col[:, HEADS + h:HEADS + h + 1]
        a_row = r8[h:h + 1, :]
        m_h = m_scr[HEADS + h:HEADS + h + 1, 0:1]
        be_h = be8[HEADS + h:HEADS + h + 1, :]

        log_d = jnp.where(causal, b_col + a_row, -jnp.inf)
        log_inter = b_col + m_h
        m_row = jnp.maximum(log_inter, jnp.max(log_d, axis=1, keepdims=True))
        w_intra = jnp.exp(log_d - m_row)
        w_inter = jnp.exp(log_inter - m_row)
        s = _dot_nt(qb, k.astype(BF16)) * w_intra
        n_h = n_scr[h:h + 1, :]
        den = (jnp.sum(s, axis=1, keepdims=True)
               + w_inter * jnp.sum(q * n_h, axis=1, keepdims=True))
        num = _dot(s.astype(BF16), v) + w_inter * _dot(qb, c_scr[h].astype(BF16))
        hh = num / jnp.maximum(jnp.abs(den), jnp.exp(-m_row))

        yield
        lte = be_h + a_col
        m_new = jnp.maximum(be_h + m_h, jnp.max(lte, axis=0, keepdims=True))
        w_src = jnp.exp(lte - m_new)
        decay = jnp.exp(be_h + m_h - m_new)
        kw = k * w_src
        c_scr[h] = decay * c_scr[h] + _dot(kw.T.astype(BF16), v)
        n_scr[h:h + 1, :] = decay * n_h + jnp.sum(kw, axis=0, keepdims=True)
        m_scr[HEADS + h:HEADS + h + 1, :] = jnp.broadcast_to(m_new, (1, LANES))
        yield

        vs = slice(h * DV, (h + 1) * DV)
        o_gate = _dot(u, wml_ref[:, OFF_G + h * DV:OFF_G + (h + 1) * DV])
        y = _head_norm(hh, mlg_ref[:, vs]) * _sigmoid(o_gate)
        mixed_scr[:, vs] = y.astype(BF16)
        yield

    qk_scr[0:SUBLANES, :] = qk_scr[T:T + SUBLANES, :]

    h1_ref[...] = x + _dot(mixed_scr[...], wout_ref[...])


def _mixer_kernel(T, G, has_pad, emit_state, n_cast, *refs):
    x_ref = refs[0]
    w_refs = refs[1:10]
    cos_ref, sin_ref, c0_ref, n0_ref, m0_ref, s0_ref, cc0_ref = refs[10:17]
    cast_in = refs[17:17 + n_cast]
    refs = refs[17 + n_cast:]
    h1_ref = refs[0]
    cast_out = refs[1:1 + n_cast]
    refs = refs[1 + n_cast:]
    if emit_state:
        c_out, n_out, m_out, s_out, cc_out = refs[:5]
        refs = refs[5:]
    c_scr, n_scr, m_scr, s_scr, qk_scr, dmat_scr, qd_scr, kd_scr, mixed_scr = refs

    b = pl.program_id(0)
    t = pl.program_id(1)
    n_t = pl.num_programs(1)

    @pl.when(t == 0)
    def _load_state():
        for g in range(G):
            c_scr[g] = c0_ref[...]
            n_scr[g] = n0_ref[...]
            m_scr[g] = m0_ref[...]
            s_scr[g] = s0_ref[...]
            qk_scr[g, 0:SUBLANES, :] = cc0_ref[...]

    @pl.when((b == 0) & (t == 0))
    def _build_decay_tables():
        ii = lax.broadcasted_iota(jnp.int32, (T, T), 0)
        jj = lax.broadcasted_iota(jnp.int32, (T, T), 1)
        rel = (ii - jj).astype(F32)
        ri = lax.broadcasted_iota(jnp.int32, (T, LANES), 0).astype(F32)
        for h in range(HEADS):
            lg = LOG_GAMMA[h]
            dmat_scr[h] = jnp.where(rel >= 0.0, jnp.exp(lg * jnp.maximum(rel, 0.0)), 0.0)
            qd_scr[h] = jnp.exp(lg * (ri + 1.0))
            kd_scr[h] = jnp.exp(lg * (T - 1.0 - ri))

    for src, dst in zip(cast_in, cast_out):
        dst[...] = src[...].astype(BF16)

    _trace_alternately([
        _mixer_tile(T, has_pad, x_ref.at[g], h1_ref.at[g], w_refs,
                    (cos_ref, sin_ref, dmat_scr, qd_scr, kd_scr),
                    (c_scr.at[g], n_scr.at[g], m_scr.at[g], s_scr.at[g]),
                    qk_scr.at[g], mixed_scr.at[g]) for g in range(G)])

    if emit_state:
        @pl.when(t == n_t - 1)
        def _store_state():
            c_out[...] = c_scr[0]
            n_out[...] = n_scr[0]
            m_out[...] = m_scr[0]
            s_out[...] = s_scr[0]
            cc_out[...] = qk_scr[0, 0:SUBLANES, :]


def _mixer_call(x3, weights, tables, state, T, G, has_pad, emit_state, cast_weights=()):
    B, L, _ = x3.shape
    n_t = L // T
    gmix, wml, wrt, wg, convw, bias8, mlg, rtg, wout = weights
    cos_t, sin_t = tables
    c0, n0, m0, s0, cc0 = state
    const2 = lambda b, t: (0, 0)
    const3 = lambda b, t: (0, 0, 0)
    in_specs = [
        pl.BlockSpec((G, T, D_MODEL), lambda b, t: (b, t, 0)),
        pl.BlockSpec(gmix.shape, const2),
        pl.BlockSpec(wml.shape, const2),
        pl.BlockSpec(wrt.shape, const2),
        pl.BlockSpec(wg.shape, const2),
        pl.BlockSpec(convw.shape, const2),
        pl.BlockSpec(bias8.shape, const2),
        pl.BlockSpec(mlg.shape, const2),
        pl.BlockSpec(rtg.shape, const2),
        pl.BlockSpec(wout.shape, const2),
        pl.BlockSpec((T, LANES), lambda b, t: (t, 0)),
        pl.BlockSpec((T, LANES), lambda b, t: (t, 0)),
        pl.BlockSpec(c0.shape, const3),
        pl.BlockSpec(n0.shape, const2),
        pl.BlockSpec(m0.shape, const2),
        pl.BlockSpec(s0.shape, const3),
        pl.BlockSpec(cc0.shape, const2),
    ]
    out_shape = [jax.ShapeDtypeStruct((B, L, D_MODEL), F32)]
    out_specs = [pl.BlockSpec((G, T, D_MODEL), lambda b, t: (b, t, 0))]
    cast_map = lambda b, t: (jnp.where(b == 0, t, n_t - 1), 0)
    for wf in cast_weights:
        rows = wf.shape[0] // n_t
        assert rows * n_t == wf.shape[0] and rows % BF16_SUBLANES == 0
        in_specs.append(pl.BlockSpec((rows, wf.shape[1]), cast_map))
        out_shape.append(jax.ShapeDtypeStruct(wf.shape, BF16))
        out_specs.append(pl.BlockSpec((rows, wf.shape[1]), cast_map))
    if emit_state:
        for a, cm in ((c0, const3), (n0, const2), (m0, const2), (s0, const3), (cc0, const2)):
            out_shape.append(jax.ShapeDtypeStruct(a.shape, F32))
            out_specs.append(pl.BlockSpec(a.shape, cm))
    scratch = [
        pltpu.VMEM((G,) + c0.shape, F32),
        pltpu.VMEM((G,) + n0.shape, F32),
        pltpu.VMEM((G,) + m0.shape, F32),
        pltpu.VMEM((G,) + s0.shape, F32),
        pltpu.VMEM((G, T + SUBLANES, 2 * HEADS * DQK), F32),
        pltpu.VMEM((HEADS, T, T), F32),
        pltpu.VMEM((HEADS, T, LANES), F32),
        pltpu.VMEM((HEADS, T, LANES), F32),
        pltpu.VMEM((G, T, D_MIX), BF16),
    ]
    return pl.pallas_call(
        functools.partial(_mixer_kernel, T, G, has_pad, emit_state, len(cast_weights)),
        out_shape=tuple(out_shape),
        grid=(B // G, n_t),
        in_specs=in_specs,
        out_specs=tuple(out_specs),
        scratch_shapes=scratch,
        compiler_params=pltpu.CompilerParams(
            dimension_semantics=("arbitrary", "arbitrary"),
            vmem_limit_bytes=VMEM_LIMIT_BYTES),
        name="mixer_prologue" if emit_state else "mixer",
    )(x3, gmix, wml, wrt, wg, convw, bias8, mlg, rtg, wout, cos_t, sin_t, c0, n0, m0, s0, cc0,
      *cast_weights)


def _ffn_tile(T, h1_ref, out_ref, w_refs, a_scr, z_scr):
    gffn_ref, wup_ref, wgate_ref, convw_ref, wdown_ref, gfin_ref = w_refs
    x1 = h1_ref[...]
    u = _rmsnorm(x1, gffn_ref[...]).astype(BF16)
    for c in range(D_FF // FFN_COLS):
        cols = slice(c * FFN_COLS, (c + 1) * FFN_COLS)
        a_scr[SUBLANES:SUBLANES + T, cols] = _dot(u, wup_ref[:, cols])
        a = convw_ref[FFN_CONV - 1:FFN_CONV, cols] * a_scr[SUBLANES:SUBLANES + T, cols]
        for k in range(FFN_CONV - 1):
            r0 = SUBLANES - (FFN_CONV - 1) + k
            a = a + convw_ref[k:k + 1, cols] * a_scr[r0:r0 + T, cols]
        gate = _dot(u, wgate_ref[:, cols])
        z_scr[:, cols] = (_silu(a) * gate).astype(BF16)
        yield
    a_scr[0:SUBLANES, :] = a_scr[T:T + SUBLANES, :]

    y = x1 + _dot(z_scr[...], wdown_ref[...])
    out_ref[...] = _rmsnorm(y, gfin_ref[...])


def _ffn_kernel(T, G, emit_state, *refs):
    h1_ref = refs[0]
    w_refs = refs[1:7]
    ac0_ref = refs[7]
    out_ref = refs[8]
    if emit_state:
        ac_out = refs[9]
        a_scr, z_scr = refs[10:]
    else:
        a_scr, z_scr = refs[9:]
    t = pl.program_id(1)
    n_t = pl.num_programs(1)

    @pl.when(t == 0)
    def _load_state():
        for g in range(G):
            a_scr[g, 0:SUBLANES, :] = ac0_ref[...]

    _trace_alternately([_ffn_tile(T, h1_ref.at[g], out_ref.at[g], w_refs, a_scr.at[g], z_scr.at[g])
                        for g in range(G)])

    if emit_state:
        @pl.when(t == n_t - 1)
        def _store_state():
            ac_out[...] = a_scr[0, 0:SUBLANES, :]


def _ffn_call(h1, weights, ac0, T, G, emit_state):
    B, L, _ = h1.shape
    n_t = L // T
    gffn, wup, wgate, convw, wdown, gfin = weights
    const2 = lambda b, t: (0, 0)
    in_specs = [
        pl.BlockSpec((G, T, D_MODEL), lambda b, t: (b, t, 0)),
        pl.BlockSpec(gffn.shape, const2),
        pl.BlockSpec(wup.shape, const2),
        pl.BlockSpec(wgate.shape, const2),
        pl.BlockSpec(convw.shape, const2),
        pl.BlockSpec(wdown.shape, const2),
        pl.BlockSpec(gfin.shape, const2),
        pl.BlockSpec(ac0.shape, const2),
    ]
    out_shape = [jax.ShapeDtypeStruct((B, L, D_MODEL), F32)]
    out_specs = [pl.BlockSpec((G, T, D_MODEL), lambda b, t: (b, t, 0))]
    if emit_state:
        out_shape.append(jax.ShapeDtypeStruct(ac0.shape, F32))
        out_specs.append(pl.BlockSpec(ac0.shape, const2))
    return pl.pallas_call(
        functools.partial(_ffn_kernel, T, G, emit_state),
        out_shape=tuple(out_shape),
        grid=(B // G, n_t),
        in_specs=in_specs,
        out_specs=tuple(out_specs),
        scratch_shapes=[pltpu.VMEM((G, T + SUBLANES, D_FF), F32), pltpu.VMEM((G, T, D_FF), BF16)],
        compiler_params=pltpu.CompilerParams(
            dimension_semantics=("arbitrary", "arbitrary"),
            vmem_limit_bytes=VMEM_LIMIT_BYTES),
        name="ffn_prologue" if emit_state else "ffn",
    )(h1, gffn, wup, wgate, convw, wdown, gfin, ac0)


def kernel(x, meta_tokens, norm_mix_g, w_in, ml_conv_w, ml_b_i, ml_b_f, ml_norm_g, rt_norm_g, w_out,
           norm_ffn_g, w_up, w_gate, ffn_conv_w, w_down, norm_final_g):
    B, L, D = x.shape
    assert D == D_MODEL and L % TILE_MIX == 0 and L % TILE_FFN == 0
    assert w_in.shape[0] == 1, "single-layer block"

    w = w_in[0]
    n_gate = 2 * HEADS
    assert w.shape[1] == 2 * D_GROUP + n_gate
    wml = w[:, :D_GROUP].astype(BF16)
    wrt = w[:, D_GROUP + n_gate:].astype(BF16)
    wg = jnp.concatenate([w[:, D_GROUP:D_GROUP + n_gate],
                          jnp.zeros((D, LANES - n_gate), F32)], axis=1).astype(BF16)
    bias8 = jnp.broadcast_to(jnp.concatenate([ml_b_i[0], ml_b_f[0]])[:, None], (n_gate, LANES))
    mixer_w = (norm_mix_g[0][None, :], wml, wrt, wg, ml_conv_w[0], bias8,
               ml_norm_g[0][None, :], rt_norm_g[0][None, :], w_out[0].astype(BF16))

    invf = ROPE_BASE ** (-jnp.arange(0, DQK, 2, dtype=F32) / DQK)
    invf = jnp.concatenate([invf, invf])[None, :]
    tab_meta = _rope_tables(0, CHUNK, invf)
    tab_main = _rope_tables(CHUNK, L, invf)

    meta_chunk = jnp.concatenate([jnp.zeros((N_PAD, D), F32), meta_tokens.astype(F32)], axis=0)[None]
    zero_state = (jnp.zeros((HEADS, DQK, DV), F32), jnp.zeros((SUBLANES, LANES), F32),
                  jnp.full((SUBLANES, LANES), NEG, F32), jnp.zeros((HEADS, DQK, DV), F32),
                  jnp.zeros((SUBLANES, 2 * HEADS * DQK), F32))
    h1_meta, *state = _mixer_call(meta_chunk, mixer_w, tab_meta, zero_state, CHUNK, 1, True, True)

    G = BATCH_GROUP if B % BATCH_GROUP == 0 else 1
    h1, wup, wgate, wdown = _mixer_call(x, mixer_w, tab_main, tuple(state), TILE_MIX, G, False, False,
                                        cast_weights=(w_up[0], w_gate[0], w_down[0]))
    ffn_w = (norm_ffn_g[0][None, :], wup, wgate, ffn_conv_w[0], wdown, norm_final_g[None, :])
    _, ac0 = _ffn_call(h1_meta, ffn_w, jnp.zeros((SUBLANES, D_FF), F32), CHUNK, 1, True)
    (out,) = _ffn_call(h1, ffn_w, ac0, TILE_FFN, G, False)
    return out
```

```python
import functools
import math

import jax
import jax.numpy as jnp
from jax import lax
from jax.experimental import pallas as pl
from jax.experimental.pallas import tpu as pltpu

F32 = jnp.float32
BF16 = jnp.bfloat16

D_MODEL = 1024
N_META = 16
CHUNK = 64
N_PAD = CHUNK - N_META
EPS = 1e-6
NEG = -1e30
HEADS = 4
DQK = 128
DV = 256
ML_CONV = 4
GATE_CAP = 15.0
ROPE_BASE = 10000.0
D_MIX = 2 * HEADS * DV
D_FF = 2816
FFN_CONV = 3
LANES = 128
SUBLANES = 8
BF16_SUBLANES = 16

OFF_Q = 0
OFF_K = OFF_Q + HEADS * DQK
OFF_V = OFF_K + HEADS * DQK
OFF_G = OFF_V + HEADS * DV
D_GROUP = OFF_G + HEADS * DV

TILE_MIX = 256
TILE_FFN = 256
BATCH_GROUP = 2
FFN_COLS = 256
VMEM_LIMIT_BYTES = 56 * 1024 * 1024
PROLOGUE_VMEM_LIMIT_BYTES = 60 * 1024 * 1024

LOG_GAMMA = tuple(math.log1p(-(2.0 ** -(5.0 + h))) for h in range(HEADS))


def _dot(a, b):
    return jnp.dot(a, b, preferred_element_type=F32)


def _dot_nt(a, b):
    return lax.dot_general(a, b, (((1,), (1,)), ((), ())), preferred_element_type=F32)


def _sigmoid(x):
    return 0.5 + 0.5 * jnp.tanh(0.5 * x)


def _silu(x):
    hx = 0.5 * x
    return hx + hx * jnp.tanh(hx)


def _rmsnorm(x, g):
    ms = jnp.mean(x * x, axis=-1, keepdims=True)
    return x * lax.rsqrt(ms + EPS) * g


def _head_norm(h, g):
    mu = jnp.mean(h, axis=-1, keepdims=True)
    c = h - mu
    var = jnp.mean(c * c, axis=-1, keepdims=True)
    return c * lax.rsqrt(var + EPS) * g


def _cumsum_lanes_mxu(x):
    r, n = x.shape
    hi = x.astype(BF16).astype(F32)
    r1 = x - hi
    mid = r1.astype(BF16).astype(F32)
    lo = (r1 - mid).astype(BF16).astype(F32)
    jsrc = lax.broadcasted_iota(jnp.int32, (n, n), 0)
    jdst = lax.broadcasted_iota(jnp.int32, (n, n), 1)
    tri = jnp.where(jsrc <= jdst, 1.0, 0.0).astype(BF16)
    p = _dot(jnp.concatenate([hi, mid, lo], axis=0).astype(BF16), tri)
    return p[0:r] + p[r:2 * r] + p[2 * r:3 * r]


def _trace_alternately(stage_generators):
    live = list(stage_generators)
    while live:
        still = []
        for gen in live:
            try:
                next(gen)
                still.append(gen)
            except StopIteration:
                pass
        live = still


def _rope_kernel(pos0, n_blocks, invf_ref, cos_ref, sin_ref):
    invf = invf_ref[...]
    lane = lax.broadcasted_iota(jnp.int32, (CHUNK, LANES), 1)
    off = lax.broadcasted_iota(jnp.int32, (CHUNK, LANES), 0).astype(F32) * invf
    cos_o, sin_o = jnp.cos(off), jnp.sin(off)
    base = ((lax.broadcasted_iota(jnp.int32, (n_blocks, LANES), 0) * CHUNK + pos0).astype(F32) * invf)
    cos_b, sin_b = jnp.cos(base), jnp.sin(base)
    for a in range(n_blocks):
        cb = cos_b[a:a + 1, :]
        sb = sin_b[a:a + 1, :]
        rows = slice(a * CHUNK, (a + 1) * CHUNK)
        cos_ref[rows, :] = cb * cos_o - sb * sin_o
        s = sb * cos_o + cb * sin_o
        sin_ref[rows, :] = jnp.where(lane < DQK // 2, -s, s)


def _rope_tables(pos0, n_rows, invf):
    assert pos0 % CHUNK == 0 and n_rows % CHUNK == 0
    out = jax.ShapeDtypeStruct((n_rows, LANES), F32)
    return pl.pallas_call(
        functools.partial(_rope_kernel, pos0, n_rows // CHUNK),
        out_shape=(out, out),
        name="rope_tables",
    )(invf)


def _mixer_tile(T, has_pad, x_ref, h1_ref, w_refs, tab_refs, state_refs, qk_scr, mixed_scr):
    gmix_ref, wml_ref, wrt_ref, wg_ref, convw_ref, bias8_ref, mlg_ref, rtg_ref, wout_ref = w_refs
    cos_ref, sin_ref, dmat_scr, qd_scr, kd_scr = tab_refs
    c_scr, n_scr, m_scr, s_scr = state_refs

    x = x_ref[...]
    u = _rmsnorm(x, gmix_ref[...]).astype(BF16)

    qk_scr[SUBLANES:SUBLANES + T, :] = _dot(u, wml_ref[:, OFF_Q:OFF_V])

    def conv_silu(c0):
        cols = slice(c0, c0 + DQK)
        acc = convw_ref[ML_CONV - 1:ML_CONV, cols] * qk_scr[SUBLANES:SUBLANES + T, cols]
        for k in range(ML_CONV - 1):
            r0 = SUBLANES - (ML_CONV - 1) + k
            acc = acc + convw_ref[k:k + 1, cols] * qk_scr[r0:r0 + T, cols]
        return _silu(acc)

    g_pre = _dot(u, wg_ref[...])
    g8 = g_pre.T[0:SUBLANES, :] + bias8_ref[:, 0:1]
    row8 = lax.broadcasted_iota(jnp.int32, (SUBLANES, T), 0)
    lane8 = lax.broadcasted_iota(jnp.int32, (SUBLANES, T), 1)
    li8 = GATE_CAP * jnp.tanh(g8 / GATE_CAP)
    if has_pad:
        li8 = jnp.where(lane8 >= N_PAD, li8, NEG)
    lf8 = jnp.minimum(g8, 0.0) - jnp.log1p(jnp.exp(-jnp.abs(g8)))
    bc8 = _cumsum_lanes_mxu(jnp.where(row8 >= HEADS, lf8, 0.0))
    r8 = jnp.where(row8 < HEADS, li8 - pltpu.roll(bc8, HEADS, 0), bc8)
    col = jnp.concatenate([r8, jnp.zeros((LANES - SUBLANES, T), F32)], axis=0).T
    be8 = jnp.sum(jnp.where(lane8 == T - 1, bc8, 0.0), axis=1, keepdims=True)

    ii = lax.broadcasted_iota(jnp.int32, (T, T), 0)
    jj = lax.broadcasted_iota(jnp.int32, (T, T), 1)
    causal = jj <= ii
    if has_pad:
        valid_col = (lax.broadcasted_iota(jnp.int32, (T, 1), 0) >= N_PAD).astype(F32)

    yield
    rq_all = _dot(u, wrt_ref[:, OFF_Q:OFF_K])
    rk_all = _dot(u, wrt_ref[:, OFF_K:OFF_V])
    cos_t = cos_ref[...]
    sin_t = sin_ref[...]
    for h in range(HEADS):
        hs = slice(h * DQK, (h + 1) * DQK)
        tq = rq_all[:, hs]
        tk = rk_all[:, hs]
        rq = tq * cos_t + pltpu.roll(tq, DQK // 2, 1) * sin_t
        rk = (tk * cos_t + pltpu.roll(tk, DQK // 2, 1) * sin_t) * (DQK ** -0.5)
        if has_pad:
            rk = rk * valid_col
        rv = _dot(u, wrt_ref[:, OFF_V + h * DV:OFF_V + (h + 1) * DV]).astype(BF16)
        yield
        scores = _dot_nt(rq.astype(BF16), rk.astype(BF16)) * dmat_scr[h]
        hr = (_dot(scores.astype(BF16), rv)
              + _dot((rq * qd_scr[h]).astype(BF16), s_scr[h].astype(BF16)))
        s_scr[h] = (math.exp(LOG_GAMMA[h] * T) * s_scr[h]
                    + _dot((rk * kd_scr[h]).T.astype(BF16), rv))
        yield
        vs = slice(h * DV, (h + 1) * DV)
        g_gate = _dot(u, wrt_ref[:, OFF_G + h * DV:OFF_G + (h + 1) * DV])
        y = _head_norm(hr, rtg_ref[:, vs]) * _silu(g_gate)
        mixed_scr[:, HEADS * DV + h * DV:HEADS * DV + (h + 1) * DV] = y.astype(BF16)
        yield

    for h in range(HEADS):
        q = conv_silu(OFF_Q + h * DQK)
        k = conv_silu(OFF_K + h * DQK) * (DQK ** -0.5)
        v = _dot(u, wml_ref[:, OFF_V + h * DV:OFF_V + (h + 1) * DV]).astype(BF16)
        qb = q.astype(BF16)
        yield
        a_col = col[:, h:h + 1]
        b_col = col[:, HEADS + h:HEADS + h + 1]
        a_row = r8[h:h + 1, :]
        m_h = m_scr[HEADS + h:HEADS + h + 1, 0:1]
        be_h = be8[HEADS + h:HEADS + h + 1, :]

        log_d = jnp.where(causal, b_col + a_row, -jnp.inf)
        log_inter = b_col + m_h
        m_row = jnp.maximum(log_inter, jnp.max(log_d, axis=1, keepdims=True))
        w_intra = jnp.exp(log_d - m_row)
        w_inter = jnp.exp(log_inter - m_row)
        s = _dot_nt(qb, k.astype(BF16)) * w_intra
        n_h = n_scr[h:h + 1, :]
        den = (jnp.sum(s, axis=1, keepdims=True)
               + w_inter * jnp.sum(q * n_h, axis=1, keepdims=True))
        num = _dot(s.astype(BF16), v) + w_inter * _dot(qb, c_scr[h].astype(BF16))
        hh = num / jnp.maximum(jnp.abs(den), jnp.exp(-m_row))

        yield
        lte = be_h + a_col
        m_new = jnp.maximum(be_h + m_h, jnp.max(lte, axis=0, keepdims=True))
        w_src = jnp.exp(lte - m_new)
        decay = jnp.exp(be_h + m_h - m_new)
        kw = k * w_src
        c_scr[h] = decay * c_scr[h] + _dot(kw.T.astype(BF16), v)
        n_scr[h:h + 1, :] = decay * n_h + jnp.sum(kw, axis=0, keepdims=True)
        m_scr[HEADS + h:HEADS + h + 1, :] = jnp.broadcast_to(m_new, (1, LANES))
        yield

        vs = slice(h * DV, (h + 1) * DV)
        o_gate = _dot(u, wml_ref[:, OFF_G + h * DV:OFF_G + (h + 1) * DV])
        y = _head_norm(hh, mlg_ref[:, vs]) * _sigmoid(o_gate)
        mixed_scr[:, vs] = y.astype(BF16)
        yield

    qk_scr[0:SUBLANES, :] = qk_scr[T:T + SUBLANES, :]

    h1_ref[...] = x + _dot(mixed_scr[...], wout_ref[...])


def _build_decay_tables(T, dmat_scr, qd_scr, kd_scr):
    ii = lax.broadcasted_iota(jnp.int32, (T, T), 0)
    jj = lax.broadcasted_iota(jnp.int32, (T, T), 1)
    rel = (ii - jj).astype(F32)
    ri = lax.broadcasted_iota(jnp.int32, (T, LANES), 0).astype(F32)
    for h in range(HEADS):
        lg = LOG_GAMMA[h]
        dmat_scr[h] = jnp.where(rel >= 0.0, jnp.exp(lg * jnp.maximum(rel, 0.0)), 0.0)
        qd_scr[h] = jnp.exp(lg * (ri + 1.0))
        kd_scr[h] = jnp.exp(lg * (T - 1.0 - ri))


def _mixer_kernel(T, G, n_cast, *refs):
    x_ref = refs[0]
    w_refs = refs[1:10]
    cos_ref, sin_ref, c0_ref, n0_ref, m0_ref, s0_ref, cc0_ref = refs[10:17]
    cast_in = refs[17:17 + n_cast]
    h1_ref = refs[17 + n_cast]
    cast_out = refs[18 + n_cast:18 + 2 * n_cast]
    c_scr, n_scr, m_scr, s_scr, qk_scr, dmat_scr, qd_scr, kd_scr, mixed_scr = refs[18 + 2 * n_cast:]

    b = pl.program_id(0)
    t = pl.program_id(1)

    @pl.when(t == 0)
    def _load_state():
        for g in range(G):
            c_scr[g] = c0_ref[...]
            n_scr[g] = n0_ref[...]
            m_scr[g] = m0_ref[...]
            s_scr[g] = s0_ref[...]
            qk_scr[g, 0:SUBLANES, :] = cc0_ref[...]

    @pl.when((b == 0) & (t == 0))
    def _tables():
        _build_decay_tables(T, dmat_scr, qd_scr, kd_scr)

    for src, dst in zip(cast_in, cast_out):
        dst[...] = src[...].astype(BF16)

    _trace_alternately([
        _mixer_tile(T, False, x_ref.at[g], h1_ref.at[g], w_refs,
                    (cos_ref, sin_ref, dmat_scr, qd_scr, kd_scr),
                    (c_scr.at[g], n_scr.at[g], m_scr.at[g], s_scr.at[g]),
                    qk_scr.at[g], mixed_scr.at[g]) for g in range(G)])


def _mixer_prologue_kernel(x_ref, gmix_ref, w_in_t_ref, convw_ref, bias8_ref, mlg_ref, rtg_ref,
                           w_out_ref, cos_ref, sin_ref,
                           h1_ref, wml_ref, wrt_ref, wg_ref, wout_ref, c_ref, n_ref, m_ref, s_ref, cc_ref,
                           qk_scr, dmat_scr, qd_scr, kd_scr, mixed_scr):
    blk = DV
    for c in range(D_GROUP // blk):
        cols = slice(c * blk, (c + 1) * blk)
        wml_ref[:, cols] = w_in_t_ref[c * blk:(c + 1) * blk, :].T.astype(BF16)
        r0 = D_GROUP + 2 * HEADS + c * blk
        wrt_ref[:, cols] = w_in_t_ref[r0:r0 + blk, :].T.astype(BF16)
    gates_t = jnp.concatenate([w_in_t_ref[D_GROUP:D_GROUP + 2 * HEADS, :],
                               jnp.zeros((LANES - 2 * HEADS, D_MODEL), F32)], axis=0)
    wg_ref[...] = gates_t.T.astype(BF16)
    wout_ref[...] = w_out_ref[...].astype(BF16)

    c_ref[...] = jnp.zeros(c_ref.shape, F32)
    n_ref[...] = jnp.zeros(n_ref.shape, F32)
    m_ref[...] = jnp.full(m_ref.shape, NEG, F32)
    s_ref[...] = jnp.zeros(s_ref.shape, F32)
    qk_scr[0:SUBLANES, :] = jnp.zeros((SUBLANES, qk_scr.shape[1]), F32)
    _build_decay_tables(CHUNK, dmat_scr, qd_scr, kd_scr)

    w_refs = (gmix_ref, wml_ref, wrt_ref, wg_ref, convw_ref, bias8_ref, mlg_ref, rtg_ref, wout_ref)
    _trace_alternately([
        _mixer_tile(CHUNK, True, x_ref, h1_ref, w_refs, (cos_ref, sin_ref, dmat_scr, qd_scr, kd_scr),
                    (c_ref, n_ref, m_ref, s_ref), qk_scr, mixed_scr)])
    cc_ref[...] = qk_scr[0:SUBLANES, :]


def _mixer_prologue_call(meta_chunk, gmix, w_in_t, convw, bias8, mlg, rtg, w_out, tables):
    cos_t, sin_t = tables
    T = CHUNK
    f32 = lambda *shape: jax.ShapeDtypeStruct(shape, F32)
    bf16 = lambda *shape: jax.ShapeDtypeStruct(shape, BF16)
    out_shape = (f32(T, D_MODEL), bf16(D_MODEL, D_GROUP), bf16(D_MODEL, D_GROUP), bf16(D_MODEL, LANES),
                 bf16(D_MIX, D_MODEL), f32(HEADS, DQK, DV), f32(SUBLANES, LANES), f32(SUBLANES, LANES),
                 f32(HEADS, DQK, DV), f32(SUBLANES, 2 * HEADS * DQK))
    scratch = [
        pltpu.VMEM((T + SUBLANES, 2 * HEADS * DQK), F32),
        pltpu.VMEM((HEADS, T, T), F32),
        pltpu.VMEM((HEADS, T, LANES), F32),
        pltpu.VMEM((HEADS, T, LANES), F32),
        pltpu.VMEM((T, D_MIX), BF16),
    ]
    return pl.pallas_call(
        _mixer_prologue_kernel,
        out_shape=out_shape,
        scratch_shapes=scratch,
        compiler_params=pltpu.CompilerParams(vmem_limit_bytes=PROLOGUE_VMEM_LIMIT_BYTES),
        name="mixer_prologue",
    )(meta_chunk, gmix, w_in_t, convw, bias8, mlg, rtg, w_out, cos_t, sin_t)


def _mixer_call(x3, weights, tables, state, T, G, cast_weights=()):
    B, L, _ = x3.shape
    n_t = L // T
    gmix, wml, wrt, wg, convw, bias8, mlg, rtg, wout = weights
    cos_t, sin_t = tables
    c0, n0, m0, s0, cc0 = state
    const2 = lambda b, t: (0, 0)
    const3 = lambda b, t: (0, 0, 0)
    in_specs = [
        pl.BlockSpec((G, T, D_MODEL), lambda b, t: (b, t, 0)),
        pl.BlockSpec(gmix.shape, const2),
        pl.BlockSpec(wml.shape, const2),
        pl.BlockSpec(wrt.shape, const2),
        pl.BlockSpec(wg.shape, const2),
        pl.BlockSpec(convw.shape, const2),
        pl.BlockSpec(bias8.shape, const2),
        pl.BlockSpec(mlg.shape, const2),
        pl.BlockSpec(rtg.shape, const2),
        pl.BlockSpec(wout.shape, const2),
        pl.BlockSpec((T, LANES), lambda b, t: (t, 0)),
        pl.BlockSpec((T, LANES), lambda b, t: (t, 0)),
        pl.BlockSpec(c0.shape, const3),
        pl.BlockSpec(n0.shape, const2),
        pl.BlockSpec(m0.shape, const2),
        pl.BlockSpec(s0.shape, const3),
        pl.BlockSpec(cc0.shape, const2),
    ]
    out_shape = [jax.ShapeDtypeStruct((B, L, D_MODEL), F32)]
    out_specs = [pl.BlockSpec((G, T, D_MODEL), lambda b, t: (b, t, 0))]
    cast_map = lambda b, t: (jnp.where(b == 0, t, n_t - 1), 0)
    for wf in cast_weights:
        rows = wf.shape[0] // n_t
        assert rows * n_t == wf.shape[0] and rows % BF16_SUBLANES == 0
        in_specs.append(pl.BlockSpec((rows, wf.shape[1]), cast_map))
        out_shape.append(jax.ShapeDtypeStruct(wf.shape, BF16))
        out_specs.append(pl.BlockSpec((rows, wf.shape[1]), cast_map))
    scratch = [
        pltpu.VMEM((G,) + c0.shape, F32),
        pltpu.VMEM((G,) + n0.shape, F32),
        pltpu.VMEM((G,) + m0.shape, F32),
        pltpu.VMEM((G,) + s0.shape, F32),
        pltpu.VMEM((G, T + SUBLANES, 2 * HEADS * DQK), F32),
        pltpu.VMEM((HEADS, T, T), F32),
        pltpu.VMEM((HEADS, T, LANES), F32),
        pltpu.VMEM((HEADS, T, LANES), F32),
        pltpu.VMEM((G, T, D_MIX), BF16),
    ]
    return pl.pallas_call(
        functools.partial(_mixer_kernel, T, G, len(cast_weights)),
        out_shape=tuple(out_shape),
        grid=(B // G, n_t),
        in_specs=in_specs,
        out_specs=tuple(out_specs),
        scratch_shapes=scratch,
        compiler_params=pltpu.CompilerParams(
            dimension_semantics=("arbitrary", "arbitrary"),
            vmem_limit_bytes=VMEM_LIMIT_BYTES),
        name="mixer",
    )(x3, gmix, wml, wrt, wg, convw, bias8, mlg, rtg, wout, cos_t, sin_t, c0, n0, m0, s0, cc0,
      *cast_weights)


def _ffn_tile(T, h1_ref, out_ref, w_refs, a_scr, z_scr):
    gffn_ref, wup_ref, wgate_ref, convw_ref, wdown_ref, gfin_ref = w_refs
    x1 = h1_ref[...]
    u = _rmsnorm(x1, gffn_ref[...]).astype(BF16)
    for c in range(D_FF // FFN_COLS):
        cols = slice(c * FFN_COLS, (c + 1) * FFN_COLS)
        a_scr[SUBLANES:SUBLANES + T, cols] = _dot(u, wup_ref[:, cols])
        a = convw_ref[FFN_CONV - 1:FFN_CONV, cols] * a_scr[SUBLANES:SUBLANES + T, cols]
        for k in range(FFN_CONV - 1):
            r0 = SUBLANES - (FFN_CONV - 1) + k
            a = a + convw_ref[k:k + 1, cols] * a_scr[r0:r0 + T, cols]
        gate = _dot(u, wgate_ref[:, cols])
        z_scr[:, cols] = (_silu(a) * gate).astype(BF16)
        yield
    a_scr[0:SUBLANES, :] = a_scr[T:T + SUBLANES, :]

    y = x1 + _dot(z_scr[...], wdown_ref[...])
    out_ref[...] = _rmsnorm(y, gfin_ref[...])


def _ffn_kernel(T, G, emit_state, *refs):
    h1_ref = refs[0]
    w_refs = refs[1:7]
    ac0_ref = refs[7]
    out_ref = refs[8]
    if emit_state:
        ac_out = refs[9]
        a_scr, z_scr = refs[10:]
    else:
        a_scr, z_scr = refs[9:]
    t = pl.program_id(1)
    n_t = pl.num_programs(1)

    @pl.when(t == 0)
    def _load_state():
        for g in range(G):
            a_scr[g, 0:SUBLANES, :] = ac0_ref[...]

    _trace_alternately([_ffn_tile(T, h1_ref.at[g], out_ref.at[g], w_refs, a_scr.at[g], z_scr.at[g])
                        for g in range(G)])

    if emit_state:
        @pl.when(t == n_t - 1)
        def _store_state():
            ac_out[...] = a_scr[0, 0:SUBLANES, :]


def _ffn_call(h1, weights, ac0, T, G, emit_state):
    B, L, _ = h1.shape
    n_t = L // T
    gffn, wup, wgate, convw, wdown, gfin = weights
    const2 = lambda b, t: (0, 0)
    in_specs = [
        pl.BlockSpec((G, T, D_MODEL), lambda b, t: (b, t, 0)),
        pl.BlockSpec(gffn.shape, const2),
        pl.BlockSpec(wup.shape, const2),
        pl.BlockSpec(wgate.shape, const2),
        pl.BlockSpec(convw.shape, const2),
        pl.BlockSpec(wdown.shape, const2),
        pl.BlockSpec(gfin.shape, const2),
        pl.BlockSpec(ac0.shape, const2),
    ]
    out_shape = [jax.ShapeDtypeStruct((B, L, D_MODEL), F32)]
    out_specs = [pl.BlockSpec((G, T, D_MODEL), lambda b, t: (b, t, 0))]
    if emit_state:
        out_shape.append(jax.ShapeDtypeStruct(ac0.shape, F32))
        out_specs.append(pl.BlockSpec(ac0.shape, const2))
    return pl.pallas_call(
        functools.partial(_ffn_kernel, T, G, emit_state),
        out_shape=tuple(out_shape),
        grid=(B // G, n_t),
        in_specs=in_specs,
        out_specs=tuple(out_specs),
        scratch_shapes=[pltpu.VMEM((G, T + SUBLANES, D_FF), F32), pltpu.VMEM((G, T, D_FF), BF16)],
        compiler_params=pltpu.CompilerParams(
            dimension_semantics=("arbitrary", "arbitrary"),
            vmem_limit_bytes=VMEM_LIMIT_BYTES),
        name="ffn_prologue" if emit_state else "ffn",
    )(h1, gffn, wup, wgate, convw, wdown, gfin, ac0)


def kernel(x, meta_tokens, norm_mix_g, w_in, ml_conv_w, ml_b_i, ml_b_f, ml_norm_g, rt_norm_g, w_out,
           norm_ffn_g, w_up, w_gate, ffn_conv_w, w_down, norm_final_g):
    B, L, D = x.shape
    assert D == D_MODEL and L % TILE_MIX == 0 and L % TILE_FFN == 0
    assert w_in.shape[0] == 1, "single-layer block"

    w_in_t = jnp.transpose(w_in[0])
    assert w_in_t.shape[0] == 2 * D_GROUP + 2 * HEADS
    bias8 = jnp.broadcast_to(jnp.concatenate([ml_b_i[0], ml_b_f[0]])[:, None], (2 * HEADS, LANES))
    gmix, mlg, rtg = norm_mix_g[0][None, :], ml_norm_g[0][None, :], rt_norm_g[0][None, :]

    invf = ROPE_BASE ** (-jnp.arange(0, DQK, 2, dtype=F32) / DQK)
    invf = jnp.concatenate([invf, invf])[None, :]
    tab_meta = _rope_tables(0, CHUNK, invf)
    tab_main = _rope_tables(CHUNK, L, invf)

    meta_chunk = jnp.concatenate([jnp.zeros((N_PAD, D), F32), meta_tokens.astype(F32)], axis=0)
    h1_meta, wml, wrt, wg, wout, *state = _mixer_prologue_call(
        meta_chunk, gmix, w_in_t, ml_conv_w[0], bias8, mlg, rtg, w_out[0], tab_meta)
    h1_meta = h1_meta[None]
    mixer_w = (gmix, wml, wrt, wg, ml_conv_w[0], bias8, mlg, rtg, wout)

    G = BATCH_GROUP if B % BATCH_GROUP == 0 else 1
    h1, wup, wgate, wdown = _mixer_call(x, mixer_w, tab_main, tuple(state), TILE_MIX, G,
                                        cast_weights=(w_up[0], w_gate[0], w_down[0]))
    ffn_w = (norm_ffn_g[0][None, :], wup, wgate, ffn_conv_w[0], wdown, norm_final_g[None, :])
    _, ac0 = _ffn_call(h1_meta, ffn_w, jnp.zeros((SUBLANES, D_FF), F32), CHUNK, 1, True)
    (out,) = _ffn_call(h1, ffn_w, ac0, TILE_FFN, G, False)
    return out
```

```python
import functools
import math

import jax
import jax.numpy as jnp
from jax import lax
from jax.experimental import pallas as pl
from jax.experimental.pallas import tpu as pltpu

F32 = jnp.float32
BF16 = jnp.bfloat16

D_MODEL = 1024
N_META = 16
CHUNK = 64
N_PAD = CHUNK - N_META
EPS = 1e-6
NEG = -1e30
HEADS = 4
DQK = 128
DV = 256
ML_CONV = 4
GATE_CAP = 15.0
ROPE_BASE = 10000.0
D_MIX = 2 * HEADS * DV
D_FF = 2816
FFN_CONV = 3
LANES = 128
SUBLANES = 8
BF16_SUBLANES = 16

OFF_Q = 0
OFF_K = OFF_Q + HEADS * DQK
OFF_V = OFF_K + HEADS * DQK
OFF_G = OFF_V + HEADS * DV
D_GROUP = OFF_G + HEADS * DV

TILE_MIX = 256
TILE_FFN = 512
BATCH_GROUP = 2
FFN_COLS = 256
VMEM_LIMIT_BYTES = 56 * 1024 * 1024
PROLOGUE_VMEM_LIMIT_BYTES = 60 * 1024 * 1024

LOG_GAMMA = tuple(math.log1p(-(2.0 ** -(5.0 + h))) for h in range(HEADS))


def _dot(a, b):
    return jnp.dot(a, b, preferred_element_type=F32)


def _dot_nt(a, b):
    return lax.dot_general(a, b, (((1,), (1,)), ((), ())), preferred_element_type=F32)


def _sigmoid(x):
    return 0.5 + 0.5 * jnp.tanh(0.5 * x)


def _silu(x):
    hx = 0.5 * x
    return hx + hx * jnp.tanh(hx)


def _rmsnorm(x, g):
    ms = jnp.mean(x * x, axis=-1, keepdims=True)
    return x * lax.rsqrt(ms + EPS) * g


def _head_norm(h, g):
    mu = jnp.mean(h, axis=-1, keepdims=True)
    c = h - mu
    var = jnp.mean(c * c, axis=-1, keepdims=True)
    return c * lax.rsqrt(var + EPS) * g


def _cumsum_lanes_mxu(x):
    r, n = x.shape
    hi = x.astype(BF16).astype(F32)
    r1 = x - hi
    mid = r1.astype(BF16).astype(F32)
    lo = (r1 - mid).astype(BF16).astype(F32)
    jsrc = lax.broadcasted_iota(jnp.int32, (n, n), 0)
    jdst = lax.broadcasted_iota(jnp.int32, (n, n), 1)
    tri = jnp.where(jsrc <= jdst, 1.0, 0.0).astype(BF16)
    p = _dot(jnp.concatenate([hi, mid, lo], axis=0).astype(BF16), tri)
    return p[0:r] + p[r:2 * r] + p[2 * r:3 * r]


def _trace_alternately(stage_generators):
    live = list(stage_generators)
    while live:
        still = []
        for gen in live:
            try:
                next(gen)
                still.append(gen)
            except StopIteration:
                pass
        live = still


def _rope_kernel(pos0, n_blocks, invf_ref, cos_ref, sin_ref):
    invf = invf_ref[...]
    lane = lax.broadcasted_iota(jnp.int32, (CHUNK, LANES), 1)
    off = lax.broadcasted_iota(jnp.int32, (CHUNK, LANES), 0).astype(F32) * invf
    cos_o, sin_o = jnp.cos(off), jnp.sin(off)
    base = ((lax.broadcasted_iota(jnp.int32, (n_blocks, LANES), 0) * CHUNK + pos0).astype(F32) * invf)
    cos_b, sin_b = jnp.cos(base), jnp.sin(base)
    for a in range(n_blocks):
        cb = cos_b[a:a + 1, :]
        sb = sin_b[a:a + 1, :]
        rows = slice(a * CHUNK, (a + 1) * CHUNK)
        cos_ref[rows, :] = cb * cos_o - sb * sin_o
        s = sb * cos_o + cb * sin_o
        sin_ref[rows, :] = jnp.where(lane < DQK // 2, -s, s)


def _rope_tables(pos0, n_rows, invf):
    assert pos0 % CHUNK == 0 and n_rows % CHUNK == 0
    out = jax.ShapeDtypeStruct((n_rows, LANES), F32)
    return pl.pallas_call(
        functools.partial(_rope_kernel, pos0, n_rows // CHUNK),
        out_shape=(out, out),
        name="rope_tables",
    )(invf)


def _mixer_tile(T, has_pad, x_ref, h1_ref, w_refs, tab_refs, state_refs, qk_scr, mixed_scr):
    gmix_ref, wml_ref, wrt_ref, wg_ref, convw_ref, bias8_ref, mlg_ref, rtg_ref, wout_ref = w_refs
    cos_ref, sin_ref, dmat_scr, qd_scr, kd_scr = tab_refs
    c_scr, n_scr, m_scr, s_scr = state_refs

    x = x_ref[...]
    u = _rmsnorm(x, gmix_ref[...]).astype(BF16)

    qk_scr[SUBLANES:SUBLANES + T, :] = _dot(u, wml_ref[:, OFF_Q:OFF_V])

    def conv_silu(c0):
        cols = slice(c0, c0 + DQK)
        acc = convw_ref[ML_CONV - 1:ML_CONV, cols] * qk_scr[SUBLANES:SUBLANES + T, cols]
        for k in range(ML_CONV - 1):
            r0 = SUBLANES - (ML_CONV - 1) + k
            acc = acc + convw_ref[k:k + 1, cols] * qk_scr[r0:r0 + T, cols]
        return _silu(acc)

    g_pre = _dot(u, wg_ref[...])
    g8 = g_pre.T[0:SUBLANES, :] + bias8_ref[:, 0:1]
    row8 = lax.broadcasted_iota(jnp.int32, (SUBLANES, T), 0)
    lane8 = lax.broadcasted_iota(jnp.int32, (SUBLANES, T), 1)
    li8 = GATE_CAP * jnp.tanh(g8 / GATE_CAP)
    if has_pad:
        li8 = jnp.where(lane8 >= N_PAD, li8, NEG)
    lf8 = jnp.minimum(g8, 0.0) - jnp.log1p(jnp.exp(-jnp.abs(g8)))
    bc8 = _cumsum_lanes_mxu(jnp.where(row8 >= HEADS, lf8, 0.0))
    r8 = jnp.where(row8 < HEADS, li8 - pltpu.roll(bc8, HEADS, 0), bc8)
    col = jnp.concatenate([r8, jnp.zeros((LANES - SUBLANES, T), F32)], axis=0).T
    be8 = jnp.sum(jnp.where(lane8 == T - 1, bc8, 0.0), axis=1, keepdims=True)

    ii = lax.broadcasted_iota(jnp.int32, (T, T), 0)
    jj = lax.broadcasted_iota(jnp.int32, (T, T), 1)
    causal = jj <= ii
    if has_pad:
        valid_col = (lax.broadcasted_iota(jnp.int32, (T, 1), 0) >= N_PAD).astype(F32)

    yield
    rq_all = _dot(u, wrt_ref[:, OFF_Q:OFF_K])
    rk_all = _dot(u, wrt_ref[:, OFF_K:OFF_V])
    cos_t = cos_ref[...]
    sin_t = sin_ref[...]
    for h in range(HEADS):
        hs = slice(h * DQK, (h + 1) * DQK)
        tq = rq_all[:, hs]
        tk = rk_all[:, hs]
        rq = tq * cos_t + pltpu.roll(tq, DQK // 2, 1) * sin_t
        rk = (tk * cos_t + pltpu.roll(tk, DQK // 2, 1) * sin_t) * (DQK ** -0.5)
        if has_pad:
            rk = rk * valid_col
        rv = _dot(u, wrt_ref[:, OFF_V + h * DV:OFF_V + (h + 1) * DV]).astype(BF16)
        yield
        scores = _dot_nt(rq.astype(BF16), rk.astype(BF16)) * dmat_scr[h]
        hr = (_dot(scores.astype(BF16), rv)
              + _dot((rq * qd_scr[h]).astype(BF16), s_scr[h].astype(BF16)))
        s_scr[h] = (math.exp(LOG_GAMMA[h] * T) * s_scr[h]
                    + _dot((rk * kd_scr[h]).T.astype(BF16), rv))
        yield
        vs = slice(h * DV, (h + 1) * DV)
        g_gate = _dot(u, wrt_ref[:, OFF_G + h * DV:OFF_G + (h + 1) * DV])
        y = _head_norm(hr, rtg_ref[:, vs]) * _silu(g_gate)
        mixed_scr[:, HEADS * DV + h * DV:HEADS * DV + (h + 1) * DV] = y.astype(BF16)
        yield

    for h in range(HEADS):
        q = conv_silu(OFF_Q + h * DQK)
        k = conv_silu(OFF_K + h * DQK) * (DQK ** -0.5)
        v = _dot(u, wml_ref[:, OFF_V + h * DV:OFF_V + (h + 1) * DV]).astype(BF16)
        qb = q.astype(BF16)
        yield
        a_col = col[:, h:h + 1]
        b_col = col[:, HEADS + h:HEADS + h + 1]
        a_row = r8[h:h + 1, :]
        m_h = m_scr[HEADS + h:HEADS + h + 1, 0:1]
        be_h = be8[HEADS + h:HEADS + h + 1, :]

        log_d = jnp.where(causal, b_col + a_row, -jnp.inf)
        log_inter = b_col + m_h
        m_row = jnp.maximum(log_inter, jnp.max(log_d, axis=1, keepdims=True))
        w_intra = jnp.exp(log_d - m_row)
        w_inter = jnp.exp(log_inter - m_row)
        s = _dot_nt(qb, k.astype(BF16)) * w_intra
        n_h = n_scr[h:h + 1, :]
        den = (jnp.sum(s, axis=1, keepdims=True)
               + w_inter * jnp.sum(q * n_h, axis=1, keepdims=True))
        num = _dot(s.astype(BF16), v) + w_inter * _dot(qb, c_scr[h].astype(BF16))
        hh = num / jnp.maximum(jnp.abs(den), jnp.exp(-m_row))

        yield
        lte = be_h + a_col
        m_new = jnp.maximum(be_h + m_h, jnp.max(lte, axis=0, keepdims=True))
        w_src = jnp.exp(lte - m_new)
        decay = jnp.exp(be_h + m_h - m_new)
        kw = k * w_src
        c_scr[h] = decay * c_scr[h] + _dot(kw.T.astype(BF16), v)
        n_scr[h:h + 1, :] = decay * n_h + jnp.sum(kw, axis=0, keepdims=True)
        m_scr[HEADS + h:HEADS + h + 1, :] = jnp.broadcast_to(m_new, (1, LANES))
        yield

        vs = slice(h * DV, (h + 1) * DV)
        o_gate = _dot(u, wml_ref[:, OFF_G + h * DV:OFF_G + (h + 1) * DV])
        y = _head_norm(hh, mlg_ref[:, vs]) * _sigmoid(o_gate)
        mixed_scr[:, vs] = y.astype(BF16)
        yield

    qk_scr[0:SUBLANES, :] = qk_scr[T:T + SUBLANES, :]

    h1_ref[...] = x + _dot(mixed_scr[...], wout_ref[...])


def _build_decay_tables(T, dmat_scr, qd_scr, kd_scr):
    ii = lax.broadcasted_iota(jnp.int32, (T, T), 0)
    jj = lax.broadcasted_iota(jnp.int32, (T, T), 1)
    rel = (ii - jj).astype(F32)
    ri = lax.broadcasted_iota(jnp.int32, (T, LANES), 0).astype(F32)
    for h in range(HEADS):
        lg = LOG_GAMMA[h]
        dmat_scr[h] = jnp.where(rel >= 0.0, jnp.exp(lg * jnp.maximum(rel, 0.0)), 0.0)
        qd_scr[h] = jnp.exp(lg * (ri + 1.0))
        kd_scr[h] = jnp.exp(lg * (T - 1.0 - ri))


def _mixer_kernel(T, G, n_cast, *refs):
    x_ref = refs[0]
    w_refs = refs[1:10]
    cos_ref, sin_ref, c0_ref, n0_ref, m0_ref, s0_ref, cc0_ref = refs[10:17]
    cast_in = refs[17:17 + n_cast]
    h1_ref = refs[17 + n_cast]
    cast_out = refs[18 + n_cast:18 + 2 * n_cast]
    c_scr, n_scr, m_scr, s_scr, qk_scr, dmat_scr, qd_scr, kd_scr, mixed_scr = refs[18 + 2 * n_cast:]

    b = pl.program_id(0)
    t = pl.program_id(1)

    @pl.when(t == 0)
    def _load_state():
        for g in range(G):
            c_scr[g] = c0_ref[...]
            n_scr[g] = n0_ref[...]
            m_scr[g] = m0_ref[...]
            s_scr[g] = s0_ref[...]
            qk_scr[g, 0:SUBLANES, :] = cc0_ref[...]

    @pl.when((b == 0) & (t == 0))
    def _tables():
        _build_decay_tables(T, dmat_scr, qd_scr, kd_scr)

    for src, dst in zip(cast_in, cast_out):
        dst[...] = src[...].astype(BF16)

    _trace_alternately([
        _mixer_tile(T, False, x_ref.at[g], h1_ref.at[g], w_refs,
                    (cos_ref, sin_ref, dmat_scr, qd_scr, kd_scr),
                    (c_scr.at[g], n_scr.at[g], m_scr.at[g], s_scr.at[g]),
                    qk_scr.at[g], mixed_scr.at[g]) for g in range(G)])


def _mixer_prologue_kernel(x_ref, gmix_ref, w_in_t_ref, convw_ref, bias8_ref, mlg_ref, rtg_ref,
                           w_out_ref, cos_ref, sin_ref,
                           h1_ref, wml_ref, wrt_ref, wg_ref, wout_ref, c_ref, n_ref, m_ref, s_ref, cc_ref,
                           qk_scr, dmat_scr, qd_scr, kd_scr, mixed_scr):
    blk = DV
    for c in range(D_GROUP // blk):
        cols = slice(c * blk, (c + 1) * blk)
        wml_ref[:, cols] = w_in_t_ref[c * blk:(c + 1) * blk, :].T.astype(BF16)
        r0 = D_GROUP + 2 * HEADS + c * blk
        wrt_ref[:, cols] = w_in_t_ref[r0:r0 + blk, :].T.astype(BF16)
    gates_t = jnp.concatenate([w_in_t_ref[D_GROUP:D_GROUP + 2 * HEADS, :],
                               jnp.zeros((LANES - 2 * HEADS, D_MODEL), F32)], axis=0)
    wg_ref[...] = gates_t.T.astype(BF16)
    wout_ref[...] = w_out_ref[...].astype(BF16)

    c_ref[...] = jnp.zeros(c_ref.shape, F32)
    n_ref[...] = jnp.zeros(n_ref.shape, F32)
    m_ref[...] = jnp.full(m_ref.shape, NEG, F32)
    s_ref[...] = jnp.zeros(s_ref.shape, F32)
    qk_scr[0:SUBLANES, :] = jnp.zeros((SUBLANES, qk_scr.shape[1]), F32)
    _build_decay_tables(CHUNK, dmat_scr, qd_scr, kd_scr)

    w_refs = (gmix_ref, wml_ref, wrt_ref, wg_ref, convw_ref, bias8_ref, mlg_ref, rtg_ref, wout_ref)
    _trace_alternately([
        _mixer_tile(CHUNK, True, x_ref, h1_ref, w_refs, (cos_ref, sin_ref, dmat_scr, qd_scr, kd_scr),
                    (c_ref, n_ref, m_ref, s_ref), qk_scr, mixed_scr)])
    cc_ref[...] = qk_scr[0:SUBLANES, :]


def _mixer_prologue_call(meta_chunk, gmix, w_in_t, convw, bias8, mlg, rtg, w_out, tables):
    cos_t, sin_t = tables
    T = CHUNK
    f32 = lambda *shape: jax.ShapeDtypeStruct(shape, F32)
    bf16 = lambda *shape: jax.ShapeDtypeStruct(shape, BF16)
    out_shape = (f32(T, D_MODEL), bf16(D_MODEL, D_GROUP), bf16(D_MODEL, D_GROUP), bf16(D_MODEL, LANES),
                 bf16(D_MIX, D_MODEL), f32(HEADS, DQK, DV), f32(SUBLANES, LANES), f32(SUBLANES, LANES),
                 f32(HEADS, DQK, DV), f32(SUBLANES, 2 * HEADS * DQK))
    scratch = [
        pltpu.VMEM((T + SUBLANES, 2 * HEADS * DQK), F32),
        pltpu.VMEM((HEADS, T, T), F32),
        pltpu.VMEM((HEADS, T, LANES), F32),
        pltpu.VMEM((HEADS, T, LANES), F32),
        pltpu.VMEM((T, D_MIX), BF16),
    ]
    return pl.pallas_call(
        _mixer_prologue_kernel,
        out_shape=out_shape,
        scratch_shapes=scratch,
        compiler_params=pltpu.CompilerParams(vmem_limit_bytes=PROLOGUE_VMEM_LIMIT_BYTES),
        name="mixer_prologue",
    )(meta_chunk, gmix, w_in_t, convw, bias8, mlg, rtg, w_out, cos_t, sin_t)


def _mixer_call(x3, weights, tables, state, T, G, cast_weights=()):
    B, L, _ = x3.shape
    n_t = L // T
    gmix, wml, wrt, wg, convw, bias8, mlg, rtg, wout = weights
    cos_t, sin_t = tables
    c0, n0, m0, s0, cc0 = state
    const2 = lambda b, t: (0, 0)
    const3 = lambda b, t: (0, 0, 0)
    in_specs = [
        pl.BlockSpec((G, T, D_MODEL), lambda b, t: (b, t, 0)),
        pl.BlockSpec(gmix.shape, const2),
        pl.BlockSpec(wml.shape, const2),
        pl.BlockSpec(wrt.shape, const2),
        pl.BlockSpec(wg.shape, const2),
        pl.BlockSpec(convw.shape, const2),
        pl.BlockSpec(bias8.shape, const2),
        pl.BlockSpec(mlg.shape, const2),
        pl.BlockSpec(rtg.shape, const2),
        pl.BlockSpec(wout.shape, const2),
        pl.BlockSpec((T, LANES), lambda b, t: (t, 0)),
        pl.BlockSpec((T, LANES), lambda b, t: (t, 0)),
        pl.BlockSpec(c0.shape, const3),
        pl.BlockSpec(n0.shape, const2),
        pl.BlockSpec(m0.shape, const2),
        pl.BlockSpec(s0.shape, const3),
        pl.BlockSpec(cc0.shape, const2),
    ]
    out_shape = [jax.ShapeDtypeStruct((B, L, D_MODEL), F32)]
    out_specs = [pl.BlockSpec((G, T, D_MODEL), lambda b, t: (b, t, 0))]
    cast_map = lambda b, t: (jnp.where(b == 0, t, n_t - 1), 0)
    for wf in cast_weights:
        rows = wf.shape[0] // n_t
        assert rows * n_t == wf.shape[0] and rows % BF16_SUBLANES == 0
        in_specs.append(pl.BlockSpec((rows, wf.shape[1]), cast_map))
        out_shape.append(jax.ShapeDtypeStruct(wf.shape, BF16))
        out_specs.append(pl.BlockSpec((rows, wf.shape[1]), cast_map))
    scratch = [
        pltpu.VMEM((G,) + c0.shape, F32),
        pltpu.VMEM((G,) + n0.shape, F32),
        pltpu.VMEM((G,) + m0.shape, F32),
        pltpu.VMEM((G,) + s0.shape, F32),
        pltpu.VMEM((G, T + SUBLANES, 2 * HEADS * DQK), F32),
        pltpu.VMEM((HEADS, T, T), F32),
        pltpu.VMEM((HEADS, T, LANES), F32),
        pltpu.VMEM((HEADS, T, LANES), F32),
        pltpu.VMEM((G, T, D_MIX), BF16),
    ]
    return pl.pallas_call(
        functools.partial(_mixer_kernel, T, G, len(cast_weights)),
        out_shape=tuple(out_shape),
        grid=(B // G, n_t),
        in_specs=in_specs,
        out_specs=tuple(out_specs),
        scratch_shapes=scratch,
        compiler_params=pltpu.CompilerParams(
            dimension_semantics=("arbitrary", "arbitrary"),
            vmem_limit_bytes=VMEM_LIMIT_BYTES),
        name="mixer",
    )(x3, gmix, wml, wrt, wg, convw, bias8, mlg, rtg, wout, cos_t, sin_t, c0, n0, m0, s0, cc0,
      *cast_weights)


def _ffn_tile(T, h1_ref, out_ref, w_refs, a_scr, z_scr):
    gffn_ref, wup_ref, wgate_ref, convw_ref, wdown_ref, gfin_ref = w_refs
    x1 = h1_ref[...]
    u = _rmsnorm(x1, gffn_ref[...]).astype(BF16)
    for c in range(D_FF // FFN_COLS):
        cols = slice(c * FFN_COLS, (c + 1) * FFN_COLS)
        a_scr[SUBLANES:SUBLANES + T, cols] = _dot(u, wup_ref[:, cols])
        a = convw_ref[FFN_CONV - 1:FFN_CONV, cols] * a_scr[SUBLANES:SUBLANES + T, cols]
        for k in range(FFN_CONV - 1):
            r0 = SUBLANES - (FFN_CONV - 1) + k
            a = a + convw_ref[k:k + 1, cols] * a_scr[r0:r0 + T, cols]
        gate = _dot(u, wgate_ref[:, cols])
        z_scr[:, cols] = (_silu(a) * gate).astype(BF16)
        yield
    a_scr[0:SUBLANES, :] = a_scr[T:T + SUBLANES, :]

    y = x1 + _dot(z_scr[...], wdown_ref[...])
    out_ref[...] = _rmsnorm(y, gfin_ref[...])


def _ffn_kernel(T, G, h1_ref, gffn_ref, wup_ref, wgate_ref, convw_ref, wdown_ref, gfin_ref, h1_meta_ref,
                out_ref, a_scr, z_scr, carry_scr):
    b = pl.program_id(0)
    t = pl.program_id(1)

    @pl.when((b == 0) & (t == 0))
    def _meta_carry():
        tail = h1_meta_ref[CHUNK - SUBLANES:CHUNK, :]
        carry_scr[...] = _dot(_rmsnorm(tail, gffn_ref[...]).astype(BF16), wup_ref[...])

    @pl.when(t == 0)
    def _load_state():
        for g in range(G):
            a_scr[g, 0:SUBLANES, :] = carry_scr[...]

    w_refs = (gffn_ref, wup_ref, wgate_ref, convw_ref, wdown_ref, gfin_ref)
    _trace_alternately([_ffn_tile(T, h1_ref.at[g], out_ref.at[g], w_refs, a_scr.at[g], z_scr.at[g])
                        for g in range(G)])


def _ffn_call(h1, weights, h1_meta, T, G):
    B, L, _ = h1.shape
    n_t = L // T
    gffn, wup, wgate, convw, wdown, gfin = weights
    const2 = lambda b, t: (0, 0)
    resident = dict(pipeline_mode=pl.Buffered(1))
    in_specs = [
        pl.BlockSpec((G, T, D_MODEL), lambda b, t: (b, t, 0)),
        pl.BlockSpec(gffn.shape, const2),
        pl.BlockSpec(wup.shape, const2, **resident),
        pl.BlockSpec(wgate.shape, const2, **resident),
        pl.BlockSpec(convw.shape, const2),
        pl.BlockSpec(wdown.shape, const2, **resident),
        pl.BlockSpec(gfin.shape, const2),
        pl.BlockSpec(h1_meta.shape, const2),
    ]
    return pl.pallas_call(
        functools.partial(_ffn_kernel, T, G),
        out_shape=jax.ShapeDtypeStruct((B, L, D_MODEL), F32),
        grid=(B // G, n_t),
        in_specs=in_specs,
        out_specs=pl.BlockSpec((G, T, D_MODEL), lambda b, t: (b, t, 0)),
        scratch_shapes=[pltpu.VMEM((G, T + SUBLANES, D_FF), F32), pltpu.VMEM((G, T, D_FF), BF16),
                        pltpu.VMEM((SUBLANES, D_FF), F32)],
        compiler_params=pltpu.CompilerParams(
            dimension_semantics=("arbitrary", "arbitrary"),
            vmem_limit_bytes=VMEM_LIMIT_BYTES),
        name="ffn",
    )(h1, gffn, wup, wgate, convw, wdown, gfin, h1_meta)


def kernel(x, meta_tokens, norm_mix_g, w_in, ml_conv_w, ml_b_i, ml_b_f, ml_norm_g, rt_norm_g, w_out,
           norm_ffn_g, w_up, w_gate, ffn_conv_w, w_down, norm_final_g):
    B, L, D = x.shape
    assert D == D_MODEL and L % TILE_MIX == 0 and L % TILE_FFN == 0
    assert w_in.shape[0] == 1, "single-layer block"

    w_in_t = jnp.transpose(w_in[0])
    assert w_in_t.shape[0] == 2 * D_GROUP + 2 * HEADS
    bias8 = jnp.broadcast_to(jnp.concatenate([ml_b_i[0], ml_b_f[0]])[:, None], (2 * HEADS, LANES))
    gmix, mlg, rtg = norm_mix_g[0][None, :], ml_norm_g[0][None, :], rt_norm_g[0][None, :]

    invf = ROPE_BASE ** (-jnp.arange(0, DQK, 2, dtype=F32) / DQK)
    invf = jnp.concatenate([invf, invf])[None, :]
    tab_meta = _rope_tables(0, CHUNK, invf)
    tab_main = _rope_tables(CHUNK, L, invf)

    meta_chunk = jnp.concatenate([jnp.zeros((N_PAD, D), F32), meta_tokens.astype(F32)], axis=0)
    h1_meta, wml, wrt, wg, wout, *state = _mixer_prologue_call(
        meta_chunk, gmix, w_in_t, ml_conv_w[0], bias8, mlg, rtg, w_out[0], tab_meta)
    mixer_w = (gmix, wml, wrt, wg, ml_conv_w[0], bias8, mlg, rtg, wout)

    G = BATCH_GROUP if B % BATCH_GROUP == 0 else 1
    h1, wup, wgate, wdown = _mixer_call(x, mixer_w, tab_main, tuple(state), TILE_MIX, G,
                                        cast_weights=(w_up[0], w_gate[0], w_down[0]))
    ffn_w = (norm_ffn_g[0][None, :], wup, wgate, ffn_conv_w[0], wdown, norm_final_g[None, :])
    return _ffn_call(h1, ffn_w, h1_meta, TILE_FFN, G)
```

```python
import functools
import math

import jax
import jax.numpy as jnp
from jax import lax
from jax.experimental import pallas as pl
from jax.experimental.pallas import tpu as pltpu

F32 = jnp.float32
BF16 = jnp.bfloat16

D_MODEL = 1024
N_META = 16
CHUNK = 64
N_PAD = CHUNK - N_META
EPS = 1e-6
NEG = -1e30
HEADS = 4
DQK = 128
DV = 256
ML_CONV = 4
GATE_CAP = 15.0
ROPE_BASE = 10000.0
D_MIX = 2 * HEADS * DV
D_FF = 2816
FFN_CONV = 3
LANES = 128
SUBLANES = 8
BF16_SUBLANES = 16

OFF_Q = 0
OFF_K = OFF_Q + HEADS * DQK
OFF_V = OFF_K + HEADS * DQK
OFF_G = OFF_V + HEADS * DV
D_GROUP = OFF_G + HEADS * DV

TILE_MIX = 256
TILE_FFN = 512
BATCH_GROUP = 2
FFN_COLS = 256
VMEM_LIMIT_BYTES = 56 * 1024 * 1024
PROLOGUE_VMEM_LIMIT_BYTES = 60 * 1024 * 1024

LOG_GAMMA = tuple(math.log1p(-(2.0 ** -(5.0 + h))) for h in range(HEADS))


def _dot(a, b):
    return jnp.dot(a, b, preferred_element_type=F32)


def _dot_nt(a, b):
    return lax.dot_general(a, b, (((1,), (1,)), ((), ())), preferred_element_type=F32)


def _sigmoid(x):
    return 0.5 + 0.5 * jnp.tanh(0.5 * x)


def _silu(x):
    hx = 0.5 * x
    return hx + hx * jnp.tanh(hx)


def _rmsnorm(x, g):
    ms = jnp.mean(x * x, axis=-1, keepdims=True)
    return x * lax.rsqrt(ms + EPS) * g


def _head_norm(h, g):
    mu = jnp.mean(h, axis=-1, keepdims=True)
    c = h - mu
    var = jnp.mean(c * c, axis=-1, keepdims=True)
    return c * lax.rsqrt(var + EPS) * g


def _cumsum_lanes_mxu(x):
    r, n = x.shape
    hi = x.astype(BF16).astype(F32)
    r1 = x - hi
    mid = r1.astype(BF16).astype(F32)
    lo = (r1 - mid).astype(BF16).astype(F32)
    jsrc = lax.broadcasted_iota(jnp.int32, (n, n), 0)
    jdst = lax.broadcasted_iota(jnp.int32, (n, n), 1)
    tri = jnp.where(jsrc <= jdst, 1.0, 0.0).astype(BF16)
    p = _dot(jnp.concatenate([hi, mid, lo], axis=0).astype(BF16), tri)
    return p[0:r] + p[r:2 * r] + p[2 * r:3 * r]


def _trace_alternately(stage_generators):
    live = list(stage_generators)
    while live:
        still = []
        for gen in live:
            try:
                next(gen)
                still.append(gen)
            except StopIteration:
                pass
        live = still


def _rope_kernel(pos0, n_blocks, cos_ref, sin_ref):
    half = DQK // 2
    lane1 = lax.broadcasted_iota(jnp.int32, (1, LANES), 1)
    pair = jnp.where(lane1 < half, lane1, lane1 - half).astype(F32)
    invf = jnp.exp(pair * (-2.0 * math.log(ROPE_BASE) / DQK))
    lane = lax.broadcasted_iota(jnp.int32, (CHUNK, LANES), 1)
    off = lax.broadcasted_iota(jnp.int32, (CHUNK, LANES), 0).astype(F32) * invf
    cos_o, sin_o = jnp.cos(off), jnp.sin(off)
    base = ((lax.broadcasted_iota(jnp.int32, (n_blocks, LANES), 0) * CHUNK + pos0).astype(F32) * invf)
    cos_b, sin_b = jnp.cos(base), jnp.sin(base)
    for a in range(n_blocks):
        cb = cos_b[a:a + 1, :]
        sb = sin_b[a:a + 1, :]
        rows = slice(a * CHUNK, (a + 1) * CHUNK)
        cos_ref[rows, :] = cb * cos_o - sb * sin_o
        s = sb * cos_o + cb * sin_o
        sin_ref[rows, :] = jnp.where(lane < half, -s, s)


def _rope_tables(pos0, n_rows):
    assert pos0 % CHUNK == 0 and n_rows % CHUNK == 0
    out = jax.ShapeDtypeStruct((n_rows, LANES), F32)
    return pl.pallas_call(
        functools.partial(_rope_kernel, pos0, n_rows // CHUNK),
        out_shape=(out, out),
        name="rope_tables",
    )()


def _mixer_tile(T, has_pad, x_ref, h1_ref, w_refs, tab_refs, state_refs, qk_scr, mixed_scr):
    gmix_ref, wml_ref, wrt_ref, wg_ref, convw_ref, bias_ref, mlg_ref, rtg_ref, wout_ref = w_refs
    cos_ref, sin_ref, dmat_scr, qd_scr, kd_scr = tab_refs
    c_scr, n_scr, m_scr, s_scr = state_refs

    x = x_ref[...]
    u = _rmsnorm(x, gmix_ref[...]).astype(BF16)

    qk_scr[SUBLANES:SUBLANES + T, :] = _dot(u, wml_ref[:, OFF_Q:OFF_V])

    def conv_silu(c0):
        cols = slice(c0, c0 + DQK)
        acc = convw_ref[ML_CONV - 1:ML_CONV, cols] * qk_scr[SUBLANES:SUBLANES + T, cols]
        for k in range(ML_CONV - 1):
            r0 = SUBLANES - (ML_CONV - 1) + k
            acc = acc + convw_ref[k:k + 1, cols] * qk_scr[r0:r0 + T, cols]
        return _silu(acc)

    g_pre = _dot(u, wg_ref[...])
    g8 = g_pre.T[0:SUBLANES, :] + bias_ref[...]
    row8 = lax.broadcasted_iota(jnp.int32, (SUBLANES, T), 0)
    lane8 = lax.broadcasted_iota(jnp.int32, (SUBLANES, T), 1)
    li8 = GATE_CAP * jnp.tanh(g8 / GATE_CAP)
    if has_pad:
        li8 = jnp.where(lane8 >= N_PAD, li8, NEG)
    lf8 = jnp.minimum(g8, 0.0) - jnp.log1p(jnp.exp(-jnp.abs(g8)))
    bc8 = _cumsum_lanes_mxu(jnp.where(row8 >= HEADS, lf8, 0.0))
    r8 = jnp.where(row8 < HEADS, li8 - pltpu.roll(bc8, HEADS, 0), bc8)
    col = jnp.concatenate([r8, jnp.zeros((LANES - SUBLANES, T), F32)], axis=0).T
    be8 = jnp.sum(jnp.where(lane8 == T - 1, bc8, 0.0), axis=1, keepdims=True)

    ii = lax.broadcasted_iota(jnp.int32, (T, T), 0)
    jj = lax.broadcasted_iota(jnp.int32, (T, T), 1)
    causal = jj <= ii
    if has_pad:
        valid_col = (lax.broadcasted_iota(jnp.int32, (T, 1), 0) >= N_PAD).astype(F32)

    yield
    rq_all = _dot(u, wrt_ref[:, OFF_Q:OFF_K])
    rk_all = _dot(u, wrt_ref[:, OFF_K:OFF_V])
    cos_t = cos_ref[...]
    sin_t = sin_ref[...]
    for h in range(HEADS):
        hs = slice(h * DQK, (h + 1) * DQK)
        tq = rq_all[:, hs]
        tk = rk_all[:, hs]
        rq = tq * cos_t + pltpu.roll(tq, DQK // 2, 1) * sin_t
        rk = (tk * cos_t + pltpu.roll(tk, DQK // 2, 1) * sin_t) * (DQK ** -0.5)
        if has_pad:
            rk = rk * valid_col
        rv = _dot(u, wrt_ref[:, OFF_V + h * DV:OFF_V + (h + 1) * DV]).astype(BF16)
        yield
        scores = _dot_nt(rq.astype(BF16), rk.astype(BF16)) * dmat_scr[h]
        hr = (_dot(scores.astype(BF16), rv)
              + _dot((rq * qd_scr[h]).astype(BF16), s_scr[h].astype(BF16)))
        s_scr[h] = (math.exp(LOG_GAMMA[h] * T) * s_scr[h]
                    + _dot((rk * kd_scr[h]).T.astype(BF16), rv))
        yield
        vs = slice(h * DV, (h + 1) * DV)
        g_gate = _dot(u, wrt_ref[:, OFF_G + h * DV:OFF_G + (h + 1) * DV])
        y = _head_norm(hr, rtg_ref[:, vs]) * _silu(g_gate)
        mixed_scr[:, HEADS * DV + h * DV:HEADS * DV + (h + 1) * DV] = y.astype(BF16)
        yield

    for h in range(HEADS):
        q = conv_silu(OFF_Q + h * DQK)
        k = conv_silu(OFF_K + h * DQK) * (DQK ** -0.5)
        v = _dot(u, wml_ref[:, OFF_V + h * DV:OFF_V + (h + 1) * DV]).astype(BF16)
        qb = q.astype(BF16)
        yield
        a_col = col[:, h:h + 1]
        b_col = col[:, HEADS + h:HEADS + h + 1]
        a_row = r8[h:h + 1, :]
        m_h = m_scr[HEADS + h:HEADS + h + 1, 0:1]
        be_h = be8[HEADS + h:HEADS + h + 1, :]

        log_d = jnp.where(causal, b_col + a_row, -jnp.inf)
        log_inter = b_col + m_h
        m_row = jnp.maximum(log_inter, jnp.max(log_d, axis=1, keepdims=True))
        w_intra = jnp.exp(log_d - m_row)
        w_inter = jnp.exp(log_inter - m_row)
        s = _dot_nt(qb, k.astype(BF16)) * w_intra
        n_h = n_scr[h:h + 1, :]
        den = (jnp.sum(s, axis=1, keepdims=True)
               + w_inter * jnp.sum(q * n_h, axis=1, keepdims=True))
        num = _dot(s.astype(BF16), v) + w_inter * _dot(qb, c_scr[h].astype(BF16))
        hh = num / jnp.maximum(jnp.abs(den), jnp.exp(-m_row))

        yield
        lte = be_h + a_col
        m_new = jnp.maximum(be_h + m_h, jnp.max(lte, axis=0, keepdims=True))
        w_src = jnp.exp(lte - m_new)
        decay = jnp.exp(be_h + m_h - m_new)
        kw = k * w_src
        c_scr[h] = decay * c_scr[h] + _dot(kw.T.astype(BF16), v)
        n_scr[h:h + 1, :] = decay * n_h + jnp.sum(kw, axis=0, keepdims=True)
        m_scr[HEADS + h:HEADS + h + 1, :] = jnp.broadcast_to(m_new, (1, LANES))
        yield

        vs = slice(h * DV, (h + 1) * DV)
        o_gate = _dot(u, wml_ref[:, OFF_G + h * DV:OFF_G + (h + 1) * DV])
        y = _head_norm(hh, mlg_ref[:, vs]) * _sigmoid(o_gate)
        mixed_scr[:, vs] = y.astype(BF16)
        yield

    qk_scr[0:SUBLANES, :] = qk_scr[T:T + SUBLANES, :]

    h1_ref[...] = x + _dot(mixed_scr[...], wout_ref[...])


def _build_decay_tables(T, dmat_scr, qd_scr, kd_scr):
    ii = lax.broadcasted_iota(jnp.int32, (T, T), 0)
    jj = lax.broadcasted_iota(jnp.int32, (T, T), 1)
    rel = (ii - jj).astype(F32)
    ri = lax.broadcasted_iota(jnp.int32, (T, LANES), 0).astype(F32)
    for h in range(HEADS):
        lg = LOG_GAMMA[h]
        dmat_scr[h] = jnp.where(rel >= 0.0, jnp.exp(lg * jnp.maximum(rel, 0.0)), 0.0)
        qd_scr[h] = jnp.exp(lg * (ri + 1.0))
        kd_scr[h] = jnp.exp(lg * (T - 1.0 - ri))


def _mixer_kernel(T, G, n_cast, *refs):
    x_ref = refs[0]
    w_refs = refs[1:10]
    cos_ref, sin_ref, c0_ref, n0_ref, m0_ref, s0_ref, cc0_ref = refs[10:17]
    cast_in = refs[17:17 + n_cast]
    h1_ref = refs[17 + n_cast]
    cast_out = refs[18 + n_cast:18 + 2 * n_cast]
    c_scr, n_scr, m_scr, s_scr, qk_scr, dmat_scr, qd_scr, kd_scr, mixed_scr = refs[18 + 2 * n_cast:]

    b = pl.program_id(0)
    t = pl.program_id(1)

    @pl.when(t == 0)
    def _load_state():
        for g in range(G):
            c_scr[g] = c0_ref[...]
            n_scr[g] = n0_ref[...]
            m_scr[g] = m0_ref[...]
            s_scr[g] = s0_ref[...]
            qk_scr[g, 0:SUBLANES, :] = cc0_ref[...]

    @pl.when((b == 0) & (t == 0))
    def _tables():
        _build_decay_tables(T, dmat_scr, qd_scr, kd_scr)

    for src, dst in zip(cast_in, cast_out):
        dst[...] = src[...].astype(BF16)

    _trace_alternately([
        _mixer_tile(T, False, x_ref.at[g], h1_ref.at[g], w_refs,
                    (cos_ref, sin_ref, dmat_scr, qd_scr, kd_scr),
                    (c_scr.at[g], n_scr.at[g], m_scr.at[g], s_scr.at[g]),
                    qk_scr.at[g], mixed_scr.at[g]) for g in range(G)])


def _mixer_prologue_kernel(x_ref, gmix_ref, w_in_t_ref, convw_ref, bias_ref, mlg_ref, rtg_ref,
                           w_out_ref, cos_ref, sin_ref,
                           h1_ref, wml_ref, wrt_ref, wg_ref, wout_ref, c_ref, n_ref, m_ref, s_ref, cc_ref,
                           qk_scr, dmat_scr, qd_scr, kd_scr, mixed_scr, x_scr):
    x_scr[...] = jnp.concatenate([jnp.zeros((N_PAD, D_MODEL), F32), x_ref[...]], axis=0)
    blk = DV
    for c in range(D_GROUP // blk):
        cols = slice(c * blk, (c + 1) * blk)
        wml_ref[:, cols] = w_in_t_ref[c * blk:(c + 1) * blk, :].T.astype(BF16)
        r0 = D_GROUP + 2 * HEADS + c * blk
        wrt_ref[:, cols] = w_in_t_ref[r0:r0 + blk, :].T.astype(BF16)
    gates_t = jnp.concatenate([w_in_t_ref[D_GROUP:D_GROUP + 2 * HEADS, :],
                               jnp.zeros((LANES - 2 * HEADS, D_MODEL), F32)], axis=0)
    wg_ref[...] = gates_t.T.astype(BF16)
    wout_ref[...] = w_out_ref[...].astype(BF16)

    c_ref[...] = jnp.zeros(c_ref.shape, F32)
    n_ref[...] = jnp.zeros(n_ref.shape, F32)
    m_ref[...] = jnp.full(m_ref.shape, NEG, F32)
    s_ref[...] = jnp.zeros(s_ref.shape, F32)
    qk_scr[0:SUBLANES, :] = jnp.zeros((SUBLANES, qk_scr.shape[1]), F32)
    _build_decay_tables(CHUNK, dmat_scr, qd_scr, kd_scr)

    w_refs = (gmix_ref, wml_ref, wrt_ref, wg_ref, convw_ref, bias_ref, mlg_ref, rtg_ref, wout_ref)
    _trace_alternately([
        _mixer_tile(CHUNK, True, x_scr, h1_ref, w_refs, (cos_ref, sin_ref, dmat_scr, qd_scr, kd_scr),
                    (c_ref, n_ref, m_ref, s_ref), qk_scr, mixed_scr)])
    cc_ref[...] = qk_scr[0:SUBLANES, :]


def _mixer_prologue_call(meta_tokens, gmix, w_in_t, convw, bias_col, mlg, rtg, w_out, tables, table_block):
    cos_t, sin_t = tables
    T = CHUNK
    ins = (meta_tokens, gmix, w_in_t, convw, bias_col, mlg, rtg, w_out)
    whole = lambda a: pl.BlockSpec(a.shape, lambda i, nd=a.ndim: (0,) * nd)
    in_specs = [whole(a) for a in ins] + [pl.BlockSpec((T, LANES), lambda i: (table_block, 0))] * 2
    f32 = lambda *shape: jax.ShapeDtypeStruct(shape, F32)
    bf16 = lambda *shape: jax.ShapeDtypeStruct(shape, BF16)
    out_shape = (f32(T, D_MODEL), bf16(D_MODEL, D_GROUP), bf16(D_MODEL, D_GROUP), bf16(D_MODEL, LANES),
                 bf16(D_MIX, D_MODEL), f32(HEADS, DQK, DV), f32(SUBLANES, LANES), f32(SUBLANES, LANES),
                 f32(HEADS, DQK, DV), f32(SUBLANES, 2 * HEADS * DQK))
    scratch = [
        pltpu.VMEM((T + SUBLANES, 2 * HEADS * DQK), F32),
        pltpu.VMEM((HEADS, T, T), F32),
        pltpu.VMEM((HEADS, T, LANES), F32),
        pltpu.VMEM((HEADS, T, LANES), F32),
        pltpu.VMEM((T, D_MIX), BF16),
        pltpu.VMEM((T, D_MODEL), F32),
    ]
    return pl.pallas_call(
        _mixer_prologue_kernel,
        out_shape=out_shape,
        grid=(1,),
        in_specs=in_specs,
        out_specs=tuple(pl.BlockSpec(o.shape, lambda i, nd=len(o.shape): (0,) * nd) for o in out_shape),
        scratch_shapes=scratch,
        compiler_params=pltpu.CompilerParams(dimension_semantics=("arbitrary",),
                                             vmem_limit_bytes=PROLOGUE_VMEM_LIMIT_BYTES),
        name="mixer_prologue",
    )(*ins, cos_t, sin_t)


def _mixer_call(x3, weights, tables, state, T, G, cast_weights=()):
    B, L, _ = x3.shape
    n_t = L // T
    gmix, wml, wrt, wg, convw, bias_col, mlg, rtg, wout = weights
    cos_t, sin_t = tables
    c0, n0, m0, s0, cc0 = state
    const2 = lambda b, t: (0, 0)
    const3 = lambda b, t: (0, 0, 0)
    in_specs = [
        pl.BlockSpec((G, T, D_MODEL), lambda b, t: (b, t, 0)),
        pl.BlockSpec(gmix.shape, const2),
        pl.BlockSpec(wml.shape, const2),
        pl.BlockSpec(wrt.shape, const2),
        pl.BlockSpec(wg.shape, const2),
        pl.BlockSpec(convw.shape, const2),
        pl.BlockSpec(bias_col.shape, const2),
        pl.BlockSpec(mlg.shape, const2),
        pl.BlockSpec(rtg.shape, const2),
        pl.BlockSpec(wout.shape, const2),
        pl.BlockSpec((T, LANES), lambda b, t: (t + 1, 0)),
        pl.BlockSpec((T, LANES), lambda b, t: (t + 1, 0)),
        pl.BlockSpec(c0.shape, const3),
        pl.BlockSpec(n0.shape, const2),
        pl.BlockSpec(m0.shape, const2),
        pl.BlockSpec(s0.shape, const3),
        pl.BlockSpec(cc0.shape, const2),
    ]
    out_shape = [jax.ShapeDtypeStruct((B, L, D_MODEL), F32)]
    out_specs = [pl.BlockSpec((G, T, D_MODEL), lambda b, t: (b, t, 0))]
    cast_map = lambda b, t: (jnp.where(b == 0, t, n_t - 1), 0)
    for wf in cast_weights:
        rows = wf.shape[0] // n_t
        assert rows * n_t == wf.shape[0] and rows % BF16_SUBLANES == 0
        in_specs.append(pl.BlockSpec((rows, wf.shape[1]), cast_map))
        out_shape.append(jax.ShapeDtypeStruct(wf.shape, BF16))
        out_specs.append(pl.BlockSpec((rows, wf.shape[1]), cast_map))
    scratch = [
        pltpu.VMEM((G,) + c0.shape, F32),
        pltpu.VMEM((G,) + n0.shape, F32),
        pltpu.VMEM((G,) + m0.shape, F32),
        pltpu.VMEM((G,) + s0.shape, F32),
        pltpu.VMEM((G, T + SUBLANES, 2 * HEADS * DQK), F32),
        pltpu.VMEM((HEADS, T, T), F32),
        pltpu.VMEM((HEADS, T, LANES), F32),
        pltpu.VMEM((HEADS, T, LANES), F32),
        pltpu.VMEM((G, T, D_MIX), BF16),
    ]
    return pl.pallas_call(
        functools.partial(_mixer_kernel, T, G, len(cast_weights)),
        out_shape=tuple(out_shape),
        grid=(B // G, n_t),
        in_specs=in_specs,
        out_specs=tuple(out_specs),
        scratch_shapes=scratch,
        compiler_params=pltpu.CompilerParams(
            dimension_semantics=("arbitrary", "arbitrary"),
            vmem_limit_bytes=VMEM_LIMIT_BYTES),
        name="mixer",
    )(x3, gmix, wml, wrt, wg, convw, bias_col, mlg, rtg, wout, cos_t, sin_t, c0, n0, m0, s0, cc0,
      *cast_weights)


def _ffn_tile(T, h1_ref, out_ref, w_refs, a_scr, z_scr):
    gffn_ref, wup_ref, wgate_ref, convw_ref, wdown_ref, gfin_ref = w_refs
    x1 = h1_ref[...]
    u = _rmsnorm(x1, gffn_ref[...]).astype(BF16)
    for c in range(D_FF // FFN_COLS):
        cols = slice(c * FFN_COLS, (c + 1) * FFN_COLS)
        a_scr[SUBLANES:SUBLANES + T, cols] = _dot(u, wup_ref[:, cols])
        a = convw_ref[FFN_CONV - 1:FFN_CONV, cols] * a_scr[SUBLANES:SUBLANES + T, cols]
        for k in range(FFN_CONV - 1):
            r0 = SUBLANES - (FFN_CONV - 1) + k
            a = a + convw_ref[k:k + 1, cols] * a_scr[r0:r0 + T, cols]
        gate = _dot(u, wgate_ref[:, cols])
        z_scr[:, cols] = (_silu(a) * gate).astype(BF16)
        yield
    a_scr[0:SUBLANES, :] = a_scr[T:T + SUBLANES, :]

    y = x1 + _dot(z_scr[...], wdown_ref[...])
    out_ref[...] = _rmsnorm(y, gfin_ref[...])


def _ffn_kernel(T, G, h1_ref, gffn_ref, wup_ref, wgate_ref, convw_ref, wdown_ref, gfin_ref, h1_meta_ref,
                out_ref, a_scr, z_scr, carry_scr):
    b = pl.program_id(0)
    t = pl.program_id(1)

    @pl.when((b == 0) & (t == 0))
    def _meta_carry():
        tail = h1_meta_ref[CHUNK - SUBLANES:CHUNK, :]
        carry_scr[...] = _dot(_rmsnorm(tail, gffn_ref[...]).astype(BF16), wup_ref[...])

    @pl.when(t == 0)
    def _load_state():
        for g in range(G):
            a_scr[g, 0:SUBLANES, :] = carry_scr[...]

    w_refs = (gffn_ref, wup_ref, wgate_ref, convw_ref, wdown_ref, gfin_ref)
    _trace_alternately([_ffn_tile(T, h1_ref.at[g], out_ref.at[g], w_refs, a_scr.at[g], z_scr.at[g])
                        for g in range(G)])


def _ffn_call(h1, weights, h1_meta, T, G):
    B, L, _ = h1.shape
    n_t = L // T
    gffn, wup, wgate, convw, wdown, gfin = weights
    const2 = lambda b, t: (0, 0)
    resident = dict(pipeline_mode=pl.Buffered(1))
    in_specs = [
        pl.BlockSpec((G, T, D_MODEL), lambda b, t: (b, t, 0)),
        pl.BlockSpec(gffn.shape, const2),
        pl.BlockSpec(wup.shape, const2, **resident),
        pl.BlockSpec(wgate.shape, const2, **resident),
        pl.BlockSpec((None,) + convw.shape[1:], lambda b, t: (0, 0, 0)),
        pl.BlockSpec(wdown.shape, const2, **resident),
        pl.BlockSpec(gfin.shape, const2),
        pl.BlockSpec(h1_meta.shape, const2),
    ]
    return pl.pallas_call(
        functools.partial(_ffn_kernel, T, G),
        out_shape=jax.ShapeDtypeStruct((B, L, D_MODEL), F32),
        grid=(B // G, n_t),
        in_specs=in_specs,
        out_specs=pl.BlockSpec((G, T, D_MODEL), lambda b, t: (b, t, 0)),
        scratch_shapes=[pltpu.VMEM((G, T + SUBLANES, D_FF), F32), pltpu.VMEM((G, T, D_FF), BF16),
                        pltpu.VMEM((SUBLANES, D_FF), F32)],
        compiler_params=pltpu.CompilerParams(
            dimension_semantics=("arbitrary", "arbitrary"),
            vmem_limit_bytes=VMEM_LIMIT_BYTES),
        name="ffn",
    )(h1, gffn, wup, wgate, convw, wdown, gfin, h1_meta)


def kernel(x, meta_tokens, norm_mix_g, w_in, ml_conv_w, ml_b_i, ml_b_f, ml_norm_g, rt_norm_g, w_out,
           norm_ffn_g, w_up, w_gate, ffn_conv_w, w_down, norm_final_g):
    B, L, D = x.shape
    assert D == D_MODEL and L % TILE_MIX == 0 and L % TILE_FFN == 0
    assert w_in.shape[0] == 1, "single-layer block"

    w_in_t = jnp.transpose(w_in[0])
    assert w_in_t.shape[0] == 2 * D_GROUP + 2 * HEADS
    bias_col = jnp.concatenate([ml_b_i[0], ml_b_f[0]])[:, None]
    gmix, mlg, rtg = norm_mix_g[0][None, :], ml_norm_g[0][None, :], rt_norm_g[0][None, :]

    tables = _rope_tables(CHUNK - TILE_MIX, L + TILE_MIX)

    h1_meta, wml, wrt, wg, wout, *state = _mixer_prologue_call(
        meta_tokens.astype(F32), gmix, w_in_t, ml_conv_w[0], bias_col, mlg, rtg, w_out[0], tables,
        table_block=TILE_MIX // CHUNK - 1)
    mixer_w = (gmix, wml, wrt, wg, ml_conv_w[0], bias_col, mlg, rtg, wout)

    G = BATCH_GROUP if B % BATCH_GROUP == 0 else 1
    h1, wup, wgate, wdown = _mixer_call(x, mixer_w, tables, tuple(state), TILE_MIX, G,
                                        cast_weights=(w_up[0], w_gate[0], w_down[0]))
    ffn_w = (norm_ffn_g[0][None, :], wup, wgate, ffn_conv_w, wdown, norm_final_g[None, :])
    return _ffn_call(h1, ffn_w, h1_meta, TILE_FFN, G)
```

```python
import functools
import math

import jax
import jax.numpy as jnp
from jax import lax
from jax.experimental import pallas as pl
from jax.experimental.pallas import tpu as pltpu

F32 = jnp.float32
BF16 = jnp.bfloat16

D_MODEL = 1024
N_META = 16
CHUNK = 64
N_PAD = CHUNK - N_META
EPS = 1e-6
NEG = -1e30
HEADS = 4
DQK = 128
DV = 256
ML_CONV = 4
GATE_CAP = 15.0
ROPE_BASE = 10000.0
D_MIX = 2 * HEADS * DV
D_FF = 2816
FFN_CONV = 3
LANES = 128
SUBLANES = 8
BF16_SUBLANES = 16

OFF_Q = 0
OFF_K = OFF_Q + HEADS * DQK
OFF_V = OFF_K + HEADS * DQK
OFF_G = OFF_V + HEADS * DV
D_GROUP = OFF_G + HEADS * DV

TILE_MIX = 256
TILE_FFN = 512
BATCH_GROUP = 2
FFN_COLS = 256
VMEM_LIMIT_BYTES = 56 * 1024 * 1024
PROLOGUE_VMEM_LIMIT_BYTES = 60 * 1024 * 1024

LOG_GAMMA = tuple(math.log1p(-(2.0 ** -(5.0 + h))) for h in range(HEADS))


def _dot(a, b):
    return jnp.dot(a, b, preferred_element_type=F32)


def _sigmoid(x):
    return 0.5 + 0.5 * jnp.tanh(0.5 * x)


def _silu(x):
    hx = 0.5 * x
    return hx + hx * jnp.tanh(hx)


def _rmsnorm(x, g):
    ms = jnp.mean(x * x, axis=-1, keepdims=True)
    return x * lax.rsqrt(ms + EPS) * g


def _head_norm(h, g):
    mu = jnp.mean(h, axis=-1, keepdims=True)
    c = h - mu
    var = jnp.mean(c * c, axis=-1, keepdims=True)
    return c * lax.rsqrt(var + EPS) * g


def _cumsum_lanes_mxu(x):
    r, n = x.shape
    hi = x.astype(BF16).astype(F32)
    r1 = x - hi
    mid = r1.astype(BF16).astype(F32)
    lo = (r1 - mid).astype(BF16).astype(F32)
    jsrc = lax.broadcasted_iota(jnp.int32, (n, n), 0)
    jdst = lax.broadcasted_iota(jnp.int32, (n, n), 1)
    tri = jnp.where(jsrc <= jdst, 1.0, 0.0).astype(BF16)
    p = _dot(jnp.concatenate([hi, mid, lo], axis=0).astype(BF16), tri)
    return p[0:r] + p[r:2 * r] + p[2 * r:3 * r]


def _trace_alternately(stage_generators):
    live = list(stage_generators)
    while live:
        still = []
        for gen in live:
            try:
                next(gen)
                still.append(gen)
            except StopIteration:
                pass
        live = still


def _rope_kernel(pos0, n_blocks, cos_ref, sin_ref):
    half = DQK // 2
    lane1 = lax.broadcasted_iota(jnp.int32, (1, LANES), 1)
    pair = jnp.where(lane1 < half, lane1, lane1 - half).astype(F32)
    invf = jnp.exp(pair * (-2.0 * math.log(ROPE_BASE) / DQK))
    lane = lax.broadcasted_iota(jnp.int32, (CHUNK, LANES), 1)
    off = lax.broadcasted_iota(jnp.int32, (CHUNK, LANES), 0).astype(F32) * invf
    cos_o, sin_o = jnp.cos(off), jnp.sin(off)
    base = ((lax.broadcasted_iota(jnp.int32, (n_blocks, LANES), 0) * CHUNK + pos0).astype(F32) * invf)
    cos_b, sin_b = jnp.cos(base), jnp.sin(base)
    for a in range(n_blocks):
        cb = cos_b[a:a + 1, :]
        sb = sin_b[a:a + 1, :]
        rows = slice(a * CHUNK, (a + 1) * CHUNK)
        cos_ref[rows, :] = cb * cos_o - sb * sin_o
        s = sb * cos_o + cb * sin_o
        sin_ref[rows, :] = jnp.where(lane < half, -s, s)


def _rope_tables(pos0, n_rows):
    assert pos0 % CHUNK == 0 and n_rows % CHUNK == 0
    out = jax.ShapeDtypeStruct((n_rows, LANES), F32)
    return pl.pallas_call(
        functools.partial(_rope_kernel, pos0, n_rows // CHUNK),
        out_shape=(out, out),
        name="rope_tables",
    )()


def _mixer_tile(T, has_pad, x_ref, h1_ref, w_refs, tab_refs, state_refs, qk_scr, mixed_scr):
    gmix_ref, wml_ref, wrt_ref, wg_ref, convw_ref, bias_ref, mlg_ref, rtg_ref, wout_ref = w_refs
    cos_ref, sin_ref, dmat_scr, qd_scr, kd_scr = tab_refs
    c_scr, n_scr, m_scr, s_scr = state_refs

    x = x_ref[...]
    u = _rmsnorm(x, gmix_ref[...]).astype(BF16)

    qk_scr[SUBLANES:SUBLANES + T, :] = _dot(u, wml_ref[:, OFF_Q:OFF_V])

    def conv_silu(c0):
        cols = slice(c0, c0 + DQK)
        acc = convw_ref[ML_CONV - 1:ML_CONV, cols] * qk_scr[SUBLANES:SUBLANES + T, cols]
        for k in range(ML_CONV - 1):
            r0 = SUBLANES - (ML_CONV - 1) + k
            acc = acc + convw_ref[k:k + 1, cols] * qk_scr[r0:r0 + T, cols]
        return _silu(acc)

    g_pre = _dot(u, wg_ref[...])
    g8 = g_pre.T[0:SUBLANES, :] + bias_ref[...]
    row8 = lax.broadcasted_iota(jnp.int32, (SUBLANES, T), 0)
    lane8 = lax.broadcasted_iota(jnp.int32, (SUBLANES, T), 1)
    li8 = GATE_CAP * jnp.tanh(g8 / GATE_CAP)
    if has_pad:
        li8 = jnp.where(lane8 >= N_PAD, li8, NEG)
    lf8 = jnp.minimum(g8, 0.0) - jnp.log1p(jnp.exp(-jnp.abs(g8)))
    bc8 = _cumsum_lanes_mxu(jnp.where(row8 >= HEADS, lf8, 0.0))
    r8 = jnp.where(row8 < HEADS, li8 - pltpu.roll(bc8, HEADS, 0), bc8)
    col = jnp.concatenate([r8, jnp.zeros((LANES - SUBLANES, T), F32)], axis=0).T
    be8 = jnp.sum(jnp.where(lane8 == T - 1, bc8, 0.0), axis=1, keepdims=True)

    ii = lax.broadcasted_iota(jnp.int32, (T, T), 0)
    jj = lax.broadcasted_iota(jnp.int32, (T, T), 1)
    causal = jj <= ii
    if has_pad:
        valid_col = (lax.broadcasted_iota(jnp.int32, (T, 1), 0) >= N_PAD).astype(F32)

    yield
    rq_all = _dot(u, wrt_ref[:, OFF_Q:OFF_K])
    rk_all = _dot(u, wrt_ref[:, OFF_K:OFF_V])
    cos_t = cos_ref[...]
    sin_t = sin_ref[...]
    for h in range(HEADS):
        hs = slice(h * DQK, (h + 1) * DQK)
        tq = rq_all[:, hs]
        tk = rk_all[:, hs]
        rq = tq * cos_t + pltpu.roll(tq, DQK // 2, 1) * sin_t
        rk = (tk * cos_t + pltpu.roll(tk, DQK // 2, 1) * sin_t) * (DQK ** -0.5)
        if has_pad:
            rk = rk * valid_col
        rv = _dot(u, wrt_ref[:, OFF_V + h * DV:OFF_V + (h + 1) * DV]).astype(BF16)
        yield
        rk_t = rk.T
        scores = _dot(rq.astype(BF16), rk_t.astype(BF16)) * dmat_scr[h]
        hr = (_dot(scores.astype(BF16), rv)
              + _dot((rq * qd_scr[h]).astype(BF16), s_scr[h].astype(BF16)))
        s_scr[h] = (math.exp(LOG_GAMMA[h] * T) * s_scr[h]
                    + _dot((rk_t * kd_scr[h, 0:1, :]).astype(BF16), rv))
        yield
        vs = slice(h * DV, (h + 1) * DV)
        g_gate = _dot(u, wrt_ref[:, OFF_G + h * DV:OFF_G + (h + 1) * DV])
        y = _head_norm(hr, rtg_ref[:, vs]) * _silu(g_gate)
        mixed_scr[:, HEADS * DV + h * DV:HEADS * DV + (h + 1) * DV] = y.astype(BF16)
        yield

    for h in range(HEADS):
        q = conv_silu(OFF_Q + h * DQK)
        k = conv_silu(OFF_K + h * DQK) * (DQK ** -0.5)
        v = _dot(u, wml_ref[:, OFF_V + h * DV:OFF_V + (h + 1) * DV]).astype(BF16)
        qb = q.astype(BF16)
        yield
        b_col = col[:, HEADS + h:HEADS + h + 1]
        a_row = r8[h:h + 1, :]
        m_h = m_scr[HEADS + h:HEADS + h + 1, 0:1]
        be_h = be8[HEADS + h:HEADS + h + 1, :]

        log_d = jnp.where(causal, b_col + a_row, -jnp.inf)
        log_inter = b_col + m_h
        m_row = jnp.maximum(log_inter, jnp.max(log_d, axis=1, keepdims=True))
        w_intra = jnp.exp(log_d - m_row)
        w_inter = jnp.exp(log_inter - m_row)
        k_t = k.T
        s = _dot(qb, k_t.astype(BF16)) * w_intra
        n_h = n_scr[h:h + 1, :]
        den = (jnp.sum(s, axis=1, keepdims=True)
               + w_inter * jnp.sum(q * n_h, axis=1, keepdims=True))
        num = _dot(s.astype(BF16), v) + w_inter * _dot(qb, c_scr[h].astype(BF16))
        hh = num / jnp.maximum(jnp.abs(den), jnp.exp(-m_row))

        yield
        lte = be_h + a_row
        m_new = jnp.maximum(be_h + m_h, jnp.max(lte, axis=1, keepdims=True))
        w_src = jnp.exp(lte - m_new)
        decay = jnp.exp(be_h + m_h - m_new)
        c_scr[h] = decay * c_scr[h] + _dot((k_t * w_src).astype(BF16), v)
        w_rows = jnp.broadcast_to(w_src, (SUBLANES, T)).astype(BF16)
        n_scr[h:h + 1, :] = decay * n_h + _dot(w_rows, k.astype(BF16))[0:1, :]
        m_scr[HEADS + h:HEADS + h + 1, :] = jnp.broadcast_to(m_new, (1, LANES))
        yield

        vs = slice(h * DV, (h + 1) * DV)
        o_gate = _dot(u, wml_ref[:, OFF_G + h * DV:OFF_G + (h + 1) * DV])
        y = _head_norm(hh, mlg_ref[:, vs]) * _sigmoid(o_gate)
        mixed_scr[:, vs] = y.astype(BF16)
        yield

    qk_scr[0:SUBLANES, :] = qk_scr[T:T + SUBLANES, :]

    h1_ref[...] = x + _dot(mixed_scr[...], wout_ref[...])


def _build_decay_tables(T, dmat_scr, qd_scr, kd_scr):
    ii = lax.broadcasted_iota(jnp.int32, (T, T), 0)
    jj = lax.broadcasted_iota(jnp.int32, (T, T), 1)
    rel = (ii - jj).astype(F32)
    ri = lax.broadcasted_iota(jnp.int32, (T, LANES), 0).astype(F32)
    rj = lax.broadcasted_iota(jnp.int32, (SUBLANES, T), 1).astype(F32)
    for h in range(HEADS):
        lg = LOG_GAMMA[h]
        dmat_scr[h] = jnp.where(rel >= 0.0, jnp.exp(lg * jnp.maximum(rel, 0.0)), 0.0)
        qd_scr[h] = jnp.exp(lg * (ri + 1.0))
        kd_scr[h] = jnp.exp(lg * (T - 1.0 - rj))


def _mixer_kernel(T, G, n_cast, *refs):
    x_ref = refs[0]
    w_refs = refs[1:10]
    cos_ref, sin_ref, c0_ref, n0_ref, m0_ref, s0_ref, cc0_ref = refs[10:17]
    cast_in = refs[17:17 + n_cast]
    h1_ref = refs[17 + n_cast]
    cast_out = refs[18 + n_cast:18 + 2 * n_cast]
    c_scr, n_scr, m_scr, s_scr, qk_scr, dmat_scr, qd_scr, kd_scr, mixed_scr = refs[18 + 2 * n_cast:]

    b = pl.program_id(0)
    t = pl.program_id(1)

    @pl.when(t == 0)
    def _load_state():
        for g in range(G):
            c_scr[g] = c0_ref[...]
            n_scr[g] = n0_ref[...]
            m_scr[g] = m0_ref[...]
            s_scr[g] = s0_ref[...]
            qk_scr[g, 0:SUBLANES, :] = cc0_ref[...]

    @pl.when((b == 0) & (t == 0))
    def _tables():
        _build_decay_tables(T, dmat_scr, qd_scr, kd_scr)

    for src, dst in zip(cast_in, cast_out):
        dst[...] = src[...].astype(BF16)

    _trace_alternately([
        _mixer_tile(T, False, x_ref.at[g], h1_ref.at[g], w_refs,
                    (cos_ref, sin_ref, dmat_scr, qd_scr, kd_scr),
                    (c_scr.at[g], n_scr.at[g], m_scr.at[g], s_scr.at[g]),
                    qk_scr.at[g], mixed_scr.at[g]) for g in range(G)])


def _mixer_prologue_kernel(x_ref, gmix_ref, w_in_t_ref, convw_ref, bias_ref, mlg_ref, rtg_ref,
                           w_out_ref, cos_ref, sin_ref,
                           h1_ref, wml_ref, wrt_ref, wg_ref, wout_ref, c_ref, n_ref, m_ref, s_ref, cc_ref,
                           qk_scr, dmat_scr, qd_scr, kd_scr, mixed_scr, x_scr):
    x_scr[...] = jnp.concatenate([jnp.zeros((N_PAD, D_MODEL), F32), x_ref[...]], axis=0)
    blk = DV
    for c in range(D_GROUP // blk):
        cols = slice(c * blk, (c + 1) * blk)
        wml_ref[:, cols] = w_in_t_ref[c * blk:(c + 1) * blk, :].T.astype(BF16)
        r0 = D_GROUP + 2 * HEADS + c * blk
        wrt_ref[:, cols] = w_in_t_ref[r0:r0 + blk, :].T.astype(BF16)
    gates_t = jnp.concatenate([w_in_t_ref[D_GROUP:D_GROUP + 2 * HEADS, :],
                               jnp.zeros((LANES - 2 * HEADS, D_MODEL), F32)], axis=0)
    wg_ref[...] = gates_t.T.astype(BF16)
    wout_ref[...] = w_out_ref[...].astype(BF16)

    c_ref[...] = jnp.zeros(c_ref.shape, F32)
    n_ref[...] = jnp.zeros(n_ref.shape, F32)
    m_ref[...] = jnp.full(m_ref.shape, NEG, F32)
    s_ref[...] = jnp.zeros(s_ref.shape, F32)
    qk_scr[0:SUBLANES, :] = jnp.zeros((SUBLANES, qk_scr.shape[1]), F32)
    _build_decay_tables(CHUNK, dmat_scr, qd_scr, kd_scr)

    w_refs = (gmix_ref, wml_ref, wrt_ref, wg_ref, convw_ref, bias_ref, mlg_ref, rtg_ref, wout_ref)
    _trace_alternately([
        _mixer_tile(CHUNK, True, x_scr, h1_ref, w_refs, (cos_ref, sin_ref, dmat_scr, qd_scr, kd_scr),
                    (c_ref, n_ref, m_ref, s_ref), qk_scr, mixed_scr)])
    cc_ref[...] = qk_scr[0:SUBLANES, :]


def _mixer_prologue_call(meta_tokens, gmix, w_in_t, convw, bias_col, mlg, rtg, w_out, tables, table_block):
    cos_t, sin_t = tables
    T = CHUNK
    ins = (meta_tokens, gmix, w_in_t, convw, bias_col, mlg, rtg, w_out)
    whole = lambda a: pl.BlockSpec(a.shape, lambda i, nd=a.ndim: (0,) * nd)
    in_specs = [whole(a) for a in ins] + [pl.BlockSpec((T, LANES), lambda i: (table_block, 0))] * 2
    f32 = lambda *shape: jax.ShapeDtypeStruct(shape, F32)
    bf16 = lambda *shape: jax.ShapeDtypeStruct(shape, BF16)
    out_shape = (f32(T, D_MODEL), bf16(D_MODEL, D_GROUP), bf16(D_MODEL, D_GROUP), bf16(D_MODEL, LANES),
                 bf16(D_MIX, D_MODEL), f32(HEADS, DQK, DV), f32(SUBLANES, LANES), f32(SUBLANES, LANES),
                 f32(HEADS, DQK, DV), f32(SUBLANES, 2 * HEADS * DQK))
    scratch = [
        pltpu.VMEM((T + SUBLANES, 2 * HEADS * DQK), F32),
        pltpu.VMEM((HEADS, T, T), F32),
        pltpu.VMEM((HEADS, T, LANES), F32),
        pltpu.VMEM((HEADS, SUBLANES, T), F32),
        pltpu.VMEM((T, D_MIX), BF16),
        pltpu.VMEM((T, D_MODEL), F32),
    ]
    return pl.pallas_call(
        _mixer_prologue_kernel,
        out_shape=out_shape,
        grid=(1,),
        in_specs=in_specs,
        out_specs=tuple(pl.BlockSpec(o.shape, lambda i, nd=len(o.shape): (0,) * nd) for o in out_shape),
        scratch_shapes=scratch,
        compiler_params=pltpu.CompilerParams(dimension_semantics=("arbitrary",),
                                             vmem_limit_bytes=PROLOGUE_VMEM_LIMIT_BYTES),
        name="mixer_prologue",
    )(*ins, cos_t, sin_t)


def _mixer_call(x3, weights, tables, state, T, G, cast_weights=()):
    B, L, _ = x3.shape
    n_t = L // T
    gmix, wml, wrt, wg, convw, bias_col, mlg, rtg, wout = weights
    cos_t, sin_t = tables
    c0, n0, m0, s0, cc0 = state
    const2 = lambda b, t: (0, 0)
    const3 = lambda b, t: (0, 0, 0)
    in_specs = [
        pl.BlockSpec((G, T, D_MODEL), lambda b, t: (b, t, 0)),
        pl.BlockSpec(gmix.shape, const2),
        pl.BlockSpec(wml.shape, const2),
        pl.BlockSpec(wrt.shape, const2),
        pl.BlockSpec(wg.shape, const2),
        pl.BlockSpec(convw.shape, const2),
        pl.BlockSpec(bias_col.shape, const2),
        pl.BlockSpec(mlg.shape, const2),
        pl.BlockSpec(rtg.shape, const2),
        pl.BlockSpec(wout.shape, const2),
        pl.BlockSpec((T, LANES), lambda b, t: (t + 1, 0)),
        pl.BlockSpec((T, LANES), lambda b, t: (t + 1, 0)),
        pl.BlockSpec(c0.shape, const3),
        pl.BlockSpec(n0.shape, const2),
        pl.BlockSpec(m0.shape, const2),
        pl.BlockSpec(s0.shape, const3),
        pl.BlockSpec(cc0.shape, const2),
    ]
    out_shape = [jax.ShapeDtypeStruct((B, L, D_MODEL), F32)]
    out_specs = [pl.BlockSpec((G, T, D_MODEL), lambda b, t: (b, t, 0))]
    cast_map = lambda b, t: (jnp.where(b == 0, t, n_t - 1), 0)
    for wf in cast_weights:
        rows = wf.shape[0] // n_t
        assert rows * n_t == wf.shape[0] and rows % BF16_SUBLANES == 0
        in_specs.append(pl.BlockSpec((rows, wf.shape[1]), cast_map))
        out_shape.append(jax.ShapeDtypeStruct(wf.shape, BF16))
        out_specs.append(pl.BlockSpec((rows, wf.shape[1]), cast_map))
    scratch = [
        pltpu.VMEM((G,) + c0.shape, F32),
        pltpu.VMEM((G,) + n0.shape, F32),
        pltpu.VMEM((G,) + m0.shape, F32),
        pltpu.VMEM((G,) + s0.shape, F32),
        pltpu.VMEM((G, T + SUBLANES, 2 * HEADS * DQK), F32),
        pltpu.VMEM((HEADS, T, T), F32),
        pltpu.VMEM((HEADS, T, LANES), F32),
        pltpu.VMEM((HEADS, SUBLANES, T), F32),
        pltpu.VMEM((G, T, D_MIX), BF16),
    ]
    return pl.pallas_call(
        functools.partial(_mixer_kernel, T, G, len(cast_weights)),
        out_shape=tuple(out_shape),
        grid=(B // G, n_t),
        in_specs=in_specs,
        out_specs=tuple(out_specs),
        scratch_shapes=scratch,
        compiler_params=pltpu.CompilerParams(
            dimension_semantics=("arbitrary", "arbitrary"),
            vmem_limit_bytes=VMEM_LIMIT_BYTES),
        name="mixer",
    )(x3, gmix, wml, wrt, wg, convw, bias_col, mlg, rtg, wout, cos_t, sin_t, c0, n0, m0, s0, cc0,
      *cast_weights)


def _ffn_tile(T, h1_ref, out_ref, w_refs, a_scr, z_scr):
    gffn_ref, wup_ref, wgate_ref, convw_ref, wdown_ref, gfin_ref = w_refs
    x1 = h1_ref[...]
    u = _rmsnorm(x1, gffn_ref[...]).astype(BF16)
    for c in range(D_FF // FFN_COLS):
        cols = slice(c * FFN_COLS, (c + 1) * FFN_COLS)
        a_scr[SUBLANES:SUBLANES + T, cols] = _dot(u, wup_ref[:, cols])
        a = convw_ref[FFN_CONV - 1:FFN_CONV, cols] * a_scr[SUBLANES:SUBLANES + T, cols]
        for k in range(FFN_CONV - 1):
            r0 = SUBLANES - (FFN_CONV - 1) + k
            a = a + convw_ref[k:k + 1, cols] * a_scr[r0:r0 + T, cols]
        gate = _dot(u, wgate_ref[:, cols])
        z_scr[:, cols] = (_silu(a) * gate).astype(BF16)
        yield
    a_scr[0:SUBLANES, :] = a_scr[T:T + SUBLANES, :]

    y = x1 + _dot(z_scr[...], wdown_ref[...])
    out_ref[...] = _rmsnorm(y, gfin_ref[...])


def _ffn_kernel(T, G, h1_ref, gffn_ref, wup_ref, wgate_ref, convw_ref, wdown_ref, gfin_ref, h1_meta_ref,
                out_ref, a_scr, z_scr, carry_scr):
    b = pl.program_id(0)
    t = pl.program_id(1)

    @pl.when((b == 0) & (t == 0))
    def _meta_carry():
        tail = h1_meta_ref[CHUNK - SUBLANES:CHUNK, :]
        carry_scr[...] = _dot(_rmsnorm(tail, gffn_ref[...]).astype(BF16), wup_ref[...])

    @pl.when(t == 0)
    def _load_state():
        for g in range(G):
            a_scr[g, 0:SUBLANES, :] = carry_scr[...]

    w_refs = (gffn_ref, wup_ref, wgate_ref, convw_ref, wdown_ref, gfin_ref)
    _trace_alternately([_ffn_tile(T, h1_ref.at[g], out_ref.at[g], w_refs, a_scr.at[g], z_scr.at[g])
                        for g in range(G)])


def _ffn_call(h1, weights, h1_meta, T, G):
    B, L, _ = h1.shape
    n_t = L // T
    gffn, wup, wgate, convw, wdown, gfin = weights
    const2 = lambda b, t: (0, 0)
    resident = dict(pipeline_mode=pl.Buffered(1))
    in_specs = [
        pl.BlockSpec((G, T, D_MODEL), lambda b, t: (b, t, 0)),
        pl.BlockSpec(gffn.shape, const2),
        pl.BlockSpec(wup.shape, const2, **resident),
        pl.BlockSpec(wgate.shape, const2, **resident),
        pl.BlockSpec((None,) + convw.shape[1:], lambda b, t: (0, 0, 0)),
        pl.BlockSpec(wdown.shape, const2, **resident),
        pl.BlockSpec(gfin.shape, const2),
        pl.BlockSpec(h1_meta.shape, const2),
    ]
    return pl.pallas_call(
        functools.partial(_ffn_kernel, T, G),
        out_shape=jax.ShapeDtypeStruct((B, L, D_MODEL), F32),
        grid=(B // G, n_t),
        in_specs=in_specs,
        out_specs=pl.BlockSpec((G, T, D_MODEL), lambda b, t: (b, t, 0)),
        scratch_shapes=[pltpu.VMEM((G, T + SUBLANES, D_FF), F32), pltpu.VMEM((G, T, D_FF), BF16),
                        pltpu.VMEM((SUBLANES, D_FF), F32)],
        compiler_params=pltpu.CompilerParams(
            dimension_semantics=("arbitrary", "arbitrary"),
            vmem_limit_bytes=VMEM_LIMIT_BYTES),
        name="ffn",
    )(h1, gffn, wup, wgate, convw, wdown, gfin, h1_meta)


def kernel(x, meta_tokens, norm_mix_g, w_in, ml_conv_w, ml_b_i, ml_b_f, ml_norm_g, rt_norm_g, w_out,
           norm_ffn_g, w_up, w_gate, ffn_conv_w, w_down, norm_final_g):
    B, L, D = x.shape
    assert D == D_MODEL and L % TILE_MIX == 0 and L % TILE_FFN == 0
    assert w_in.shape[0] == 1, "single-layer block"

    w_in_t = jnp.transpose(w_in[0])
    assert w_in_t.shape[0] == 2 * D_GROUP + 2 * HEADS
    bias_col = jnp.concatenate([ml_b_i[0], ml_b_f[0]])[:, None]
    gmix, mlg, rtg = norm_mix_g[0][None, :], ml_norm_g[0][None, :], rt_norm_g[0][None, :]

    tables = _rope_tables(CHUNK - TILE_MIX, L + TILE_MIX)

    h1_meta, wml, wrt, wg, wout, *state = _mixer_prologue_call(
        meta_tokens.astype(F32), gmix, w_in_t, ml_conv_w[0], bias_col, mlg, rtg, w_out[0], tables,
        table_block=TILE_MIX // CHUNK - 1)
    mixer_w = (gmix, wml, wrt, wg, ml_conv_w[0], bias_col, mlg, rtg, wout)

    G = BATCH_GROUP if B % BATCH_GROUP == 0 else 1
    h1, wup, wgate, wdown = _mixer_call(x, mixer_w, tables, tuple(state), TILE_MIX, G,
                                        cast_weights=(w_up[0], w_gate[0], w_down[0]))
    ffn_w = (norm_ffn_g[0][None, :], wup, wgate, ffn_conv_w, wdown, norm_final_g[None, :])
    return _ffn_call(h1, ffn_w, h1_meta, TILE_FFN, G)
```

```python
import functools
import math

import jax
import jax.numpy as jnp
from jax import lax
from jax.experimental import pallas as pl
from jax.experimental.pallas import tpu as pltpu

F32 = jnp.float32
BF16 = jnp.bfloat16

D_MODEL = 1024
N_META = 16
CHUNK = 64
N_PAD = CHUNK - N_META
EPS = 1e-6
NEG = -1e30
HEADS = 4
DQK = 128
DV = 256
ML_CONV = 4
GATE_CAP = 15.0
ROPE_BASE = 10000.0
D_MIX = 2 * HEADS * DV
D_FF = 2816
FFN_CONV = 3
LANES = 128
SUBLANES = 8
BF16_SUBLANES = 16

OFF_Q = 0
OFF_K = OFF_Q + HEADS * DQK
OFF_V = OFF_K + HEADS * DQK
OFF_G = OFF_V + HEADS * DV
D_GROUP = OFF_G + HEADS * DV

TILE_MIX = 256
TILE_FFN = 512
BATCH_GROUP = 2
FFN_COLS = 256
WEIGHT_ROWS = 256
VMEM_LIMIT_BYTES = 56 * 1024 * 1024
MIXER_VMEM_LIMIT_BYTES = 60 * 1024 * 1024

LOG_GAMMA = tuple(math.log1p(-(2.0 ** -(5.0 + h))) for h in range(HEADS))


def _dot(a, b):
    return jnp.dot(a, b, preferred_element_type=F32)


def _dot_nt(a, b):
    return lax.dot_general(a, b, (((1,), (1,)), ((), ())), preferred_element_type=F32)


def _sigmoid(x):
    return 0.5 + 0.5 * jnp.tanh(0.5 * x)


def _silu(x):
    hx = 0.5 * x
    return hx + hx * jnp.tanh(hx)


def _rmsnorm(x, g):
    ms = jnp.mean(x * x, axis=-1, keepdims=True)
    return x * lax.rsqrt(ms + EPS) * g


def _head_norm(h, g):
    mu = jnp.mean(h, axis=-1, keepdims=True)
    c = h - mu
    var = jnp.mean(c * c, axis=-1, keepdims=True)
    return c * lax.rsqrt(var + EPS) * g


def _cumsum_lanes_mxu(x):
    r, n = x.shape
    hi = x.astype(BF16).astype(F32)
    r1 = x - hi
    mid = r1.astype(BF16).astype(F32)
    lo = (r1 - mid).astype(BF16).astype(F32)
    jsrc = lax.broadcasted_iota(jnp.int32, (n, n), 0)
    jdst = lax.broadcasted_iota(jnp.int32, (n, n), 1)
    tri = jnp.where(jsrc <= jdst, 1.0, 0.0).astype(BF16)
    p = _dot(jnp.concatenate([hi, mid, lo], axis=0).astype(BF16), tri)
    return p[0:r] + p[r:2 * r] + p[2 * r:3 * r]


def _trace_alternately(stage_generators):
    live = list(stage_generators)
    while live:
        still = []
        for gen in live:
            try:
                next(gen)
                still.append(gen)
            except StopIteration:
                pass
        live = still


def _rope_kernel(pos0, n_blocks, cos_ref, sin_ref):
    half = DQK // 2
    lane1 = lax.broadcasted_iota(jnp.int32, (1, LANES), 1)
    pair = jnp.where(lane1 < half, lane1, lane1 - half).astype(F32)
    invf = jnp.exp(pair * (-2.0 * math.log(ROPE_BASE) / DQK))
    lane = lax.broadcasted_iota(jnp.int32, (CHUNK, LANES), 1)
    off = lax.broadcasted_iota(jnp.int32, (CHUNK, LANES), 0).astype(F32) * invf
    cos_o, sin_o = jnp.cos(off), jnp.sin(off)
    base = ((lax.broadcasted_iota(jnp.int32, (n_blocks, LANES), 0) * CHUNK + pos0).astype(F32) * invf)
    cos_b, sin_b = jnp.cos(base), jnp.sin(base)
    for a in range(n_blocks):
        cb = cos_b[a:a + 1, :]
        sb = sin_b[a:a + 1, :]
        rows = slice(a * CHUNK, (a + 1) * CHUNK)
        cos_ref[rows, :] = cb * cos_o - sb * sin_o
        s = sb * cos_o + cb * sin_o
        sin_ref[rows, :] = jnp.where(lane < half, -s, s)


def _rope_tables(pos0, n_rows):
    assert pos0 % CHUNK == 0 and n_rows % CHUNK == 0
    out = jax.ShapeDtypeStruct((n_rows, LANES), F32)
    return pl.pallas_call(
        functools.partial(_rope_kernel, pos0, n_rows // CHUNK),
        out_shape=(out, out),
        name="rope_tables",
    )()


def _mixer_tile(T, has_pad, x_ref, h1_ref, w_refs, tab_refs, state_refs, qk_scr, mixed_scr):
    gmix_ref, wml_ref, wrt_ref, wg_ref, convw_ref, bias_ref, mlg_ref, rtg_ref, wout_ref = w_refs
    cos_ref, sin_ref, dmat_scr, qd_scr, kd_scr = tab_refs
    c_scr, n_scr, m_scr, s_scr = state_refs

    x = x_ref[...]
    u = _rmsnorm(x, gmix_ref[...]).astype(BF16)

    qk_scr[SUBLANES:SUBLANES + T, :] = _dot(u, wml_ref[:, OFF_Q:OFF_V])

    def conv_silu(c0):
        cols = slice(c0, c0 + DQK)
        acc = convw_ref[ML_CONV - 1:ML_CONV, cols] * qk_scr[SUBLANES:SUBLANES + T, cols]
        for k in range(ML_CONV - 1):
            r0 = SUBLANES - (ML_CONV - 1) + k
            acc = acc + convw_ref[k:k + 1, cols] * qk_scr[r0:r0 + T, cols]
        return _silu(acc)

    g_pre = _dot(u, wg_ref[...])
    g8 = g_pre.T[0:SUBLANES, :] + bias_ref[...]
    row8 = lax.broadcasted_iota(jnp.int32, (SUBLANES, T), 0)
    lane8 = lax.broadcasted_iota(jnp.int32, (SUBLANES, T), 1)
    li8 = GATE_CAP * jnp.tanh(g8 / GATE_CAP)
    if has_pad:
        li8 = jnp.where(lane8 >= N_PAD, li8, NEG)
    lf8 = jnp.minimum(g8, 0.0) - jnp.log1p(jnp.exp(-jnp.abs(g8)))
    bc8 = _cumsum_lanes_mxu(jnp.where(row8 >= HEADS, lf8, 0.0))
    r8 = jnp.where(row8 < HEADS, li8 - pltpu.roll(bc8, HEADS, 0), bc8)
    col = jnp.concatenate([r8, jnp.zeros((LANES - SUBLANES, T), F32)], axis=0).T
    be8 = jnp.sum(jnp.where(lane8 == T - 1, bc8, 0.0), axis=1, keepdims=True)

    ii = lax.broadcasted_iota(jnp.int32, (T, T), 0)
    jj = lax.broadcasted_iota(jnp.int32, (T, T), 1)
    causal = jj <= ii
    if has_pad:
        valid_col = (lax.broadcasted_iota(jnp.int32, (T, 1), 0) >= N_PAD).astype(F32)

    yield
    rq_all = _dot(u, wrt_ref[:, OFF_Q:OFF_K])
    rk_all = _dot(u, wrt_ref[:, OFF_K:OFF_V])
    cos_t = cos_ref[...]
    sin_t = sin_ref[...]
    for h in range(HEADS):
        hs = slice(h * DQK, (h + 1) * DQK)
        tq = rq_all[:, hs]
        tk = rk_all[:, hs]
        rq = tq * cos_t + pltpu.roll(tq, DQK // 2, 1) * sin_t
        rk = (tk * cos_t + pltpu.roll(tk, DQK // 2, 1) * sin_t) * (DQK ** -0.5)
        if has_pad:
            rk = rk * valid_col
        rv = _dot(u, wrt_ref[:, OFF_V + h * DV:OFF_V + (h + 1) * DV]).astype(BF16)
        yield
        scores = _dot_nt(rq.astype(BF16), rk.astype(BF16)) * dmat_scr[h]
        hr = (_dot(scores.astype(BF16), rv)
              + _dot((rq * qd_scr[h]).astype(BF16), s_scr[h].astype(BF16)))
        s_scr[h] = (math.exp(LOG_GAMMA[h] * T) * s_scr[h]
                    + _dot((rk * kd_scr[h]).T.astype(BF16), rv))
        yield
        vs = slice(h * DV, (h + 1) * DV)
        g_gate = _dot(u, wrt_ref[:, OFF_G + h * DV:OFF_G + (h + 1) * DV])
        y = _head_norm(hr, rtg_ref[:, vs]) * _silu(g_gate)
        mixed_scr[:, HEADS * DV + h * DV:HEADS * DV + (h + 1) * DV] = y.astype(BF16)
        yield

    for h in range(HEADS):
        q = conv_silu(OFF_Q + h * DQK)
        k = conv_silu(OFF_K + h * DQK) * (DQK ** -0.5)
        v = _dot(u, wml_ref[:, OFF_V + h * DV:OFF_V + (h + 1) * DV]).astype(BF16)
        qb = q.astype(BF16)
        yield
        a_col = col[:, h:h + 1]
        b_col = col[:, HEADS + h:HEADS + h + 1]
        a_row = r8[h:h + 1, :]
        m_h = m_scr[HEADS + h:HEADS + h + 1, 0:1]
        be_h = be8[HEADS + h:HEADS + h + 1, :]

        log_d = jnp.where(causal, b_col + a_row, -jnp.inf)
        log_inter = b_col + m_h
        m_row = jnp.maximum(log_inter, jnp.max(log_d, axis=1, keepdims=True))
        w_intra = jnp.exp(log_d - m_row)
        w_inter = jnp.exp(log_inter - m_row)
        s = _dot_nt(qb, k.astype(BF16)) * w_intra
        n_h = n_scr[h:h + 1, :]
        den = (jnp.sum(s, axis=1, keepdims=True)
               + w_inter * jnp.sum(q * n_h, axis=1, keepdims=True))
        num = _dot(s.astype(BF16), v) + w_inter * _dot(qb, c_scr[h].astype(BF16))
        hh = num / jnp.maximum(jnp.abs(den), jnp.exp(-m_row))

        yield
        lte = be_h + a_col
        m_new = jnp.maximum(be_h + m_h, jnp.max(lte, axis=0, keepdims=True))
        w_src = jnp.exp(lte - m_new)
        decay = jnp.exp(be_h + m_h - m_new)
        kw = k * w_src
        c_scr[h] = decay * c_scr[h] + _dot(kw.T.astype(BF16), v)
        n_scr[h:h + 1, :] = decay * n_h + jnp.sum(kw, axis=0, keepdims=True)
        m_scr[HEADS + h:HEADS + h + 1, :] = jnp.broadcast_to(m_new, (1, LANES))
        yield

        vs = slice(h * DV, (h + 1) * DV)
        o_gate = _dot(u, wml_ref[:, OFF_G + h * DV:OFF_G + (h + 1) * DV])
        y = _head_norm(hh, mlg_ref[:, vs]) * _sigmoid(o_gate)
        mixed_scr[:, vs] = y.astype(BF16)
        yield

    qk_scr[0:SUBLANES, :] = qk_scr[T:T + SUBLANES, :]

    h1_ref[...] = x + _dot(mixed_scr[...], wout_ref[...])


def _build_decay_tables(T, dmat_scr, qd_scr, kd_scr):
    ii = lax.broadcasted_iota(jnp.int32, (T, T), 0)
    jj = lax.broadcasted_iota(jnp.int32, (T, T), 1)
    rel = (ii - jj).astype(F32)
    ri = lax.broadcasted_iota(jnp.int32, (T, LANES), 0).astype(F32)
    for h in range(HEADS):
        lg = LOG_GAMMA[h]
        dmat_scr[h] = jnp.where(rel >= 0.0, jnp.exp(lg * jnp.maximum(rel, 0.0)), 0.0)
        qd_scr[h] = jnp.exp(lg * (ri + 1.0))
        kd_scr[h] = jnp.exp(lg * (T - 1.0 - ri))


def _lay_out_weights(w_in_t_hbm, w_out_hbm, wml_scr, wrt_scr, wg_scr, wout_scr, stage, sem):
    n_gate = 2 * HEADS
    rt0 = D_GROUP + n_gate
    jobs = []
    for c in range(D_GROUP // WEIGHT_ROWS):
        cols = slice(c * WEIGHT_ROWS, (c + 1) * WEIGHT_ROWS)

        def to_ml(tile, cols=cols):
            wml_scr[:, cols] = tile.T.astype(BF16)

        def to_rt(tile, cols=cols):
            wrt_scr[:, cols] = tile.T.astype(BF16)

        jobs.append((w_in_t_hbm, c * WEIGHT_ROWS, WEIGHT_ROWS, to_ml))
        jobs.append((w_in_t_hbm, rt0 + c * WEIGHT_ROWS, WEIGHT_ROWS, to_rt))

    def to_gates(tile):
        padded = jnp.concatenate([tile, jnp.zeros((LANES - n_gate, D_MODEL), F32)], axis=0)
        wg_scr[...] = padded.T.astype(BF16)

    jobs.append((w_in_t_hbm, D_GROUP, n_gate, to_gates))
    for c in range(D_MIX // WEIGHT_ROWS):
        def to_out(tile, c=c):
            wout_scr[c * WEIGHT_ROWS:(c + 1) * WEIGHT_ROWS, :] = tile.astype(BF16)

        jobs.append((w_out_hbm, c * WEIGHT_ROWS, WEIGHT_ROWS, to_out))

    def copy(i):
        src, r0, rows, _ = jobs[i]
        slot = i % 2
        return pltpu.make_async_copy(src.at[pl.ds(r0, rows), :], stage.at[slot, pl.ds(0, rows), :],
                                     sem.at[slot])

    copy(0).start()
    for i, (_, _, rows, consume) in enumerate(jobs):
        if i + 1 < len(jobs):
            copy(i + 1).start()
        copy(i).wait()
        consume(stage[i % 2, 0:rows, :])


def _mixer_kernel(T, G, n_cast, *refs):
    (x_ref, gmix_ref, w_in_t_hbm, convw_ref, bias_ref, mlg_ref, rtg_ref, w_out_hbm,
     cos_ref, sin_ref, cos0_ref, sin0_ref, meta_ref) = refs[:13]
    cast_in = refs[13:13 + n_cast]
    h1_ref, h1_meta_ref = refs[13 + n_cast:15 + n_cast]
    cast_out = refs[15 + n_cast:15 + 2 * n_cast]
    (c_scr, n_scr, m_scr, s_scr, qk_scr, dmat_scr, qd_scr, kd_scr, mixed_scr,
     wml_scr, wrt_scr, wg_scr, wout_scr, stage, sem,
     c_init, n_init, m_init, s_init, cc_init,
     qk0_scr, dmat0_scr, qd0_scr, kd0_scr, mixed0_scr, x0_scr) = refs[15 + 2 * n_cast:]
    w_refs = (gmix_ref, wml_scr, wrt_scr, wg_scr, convw_ref, bias_ref, mlg_ref, rtg_ref, wout_scr)

    b = pl.program_id(0)
    t = pl.program_id(1)

    @pl.when((b == 0) & (t == 0))
    def _prologue():
        _lay_out_weights(w_in_t_hbm, w_out_hbm, wml_scr, wrt_scr, wg_scr, wout_scr, stage, sem)
        x0_scr[...] = jnp.concatenate([jnp.zeros((N_PAD, D_MODEL), F32), meta_ref[...]], axis=0)
        c_init[...] = jnp.zeros(c_init.shape, F32)
        n_init[...] = jnp.zeros(n_init.shape, F32)
        m_init[...] = jnp.full(m_init.shape, NEG, F32)
        s_init[...] = jnp.zeros(s_init.shape, F32)
        qk0_scr[0:SUBLANES, :] = jnp.zeros((SUBLANES, qk0_scr.shape[1]), F32)
        _build_decay_tables(CHUNK, dmat0_scr, qd0_scr, kd0_scr)
        _trace_alternately([
            _mixer_tile(CHUNK, True, x0_scr, h1_meta_ref, w_refs,
                        (cos0_ref, sin0_ref, dmat0_scr, qd0_scr, kd0_scr),
                        (c_init, n_init, m_init, s_init), qk0_scr, mixed0_scr)])
        cc_init[...] = qk0_scr[0:SUBLANES, :]
        _build_decay_tables(T, dmat_scr, qd_scr, kd_scr)

    @pl.when(t == 0)
    def _load_state():
        for g in range(G):
            c_scr[g] = c_init[...]
            n_scr[g] = n_init[...]
            m_scr[g] = m_init[...]
            s_scr[g] = s_init[...]
            qk_scr[g, 0:SUBLANES, :] = cc_init[...]

    for src, dst in zip(cast_in, cast_out):
        dst[...] = src[...].astype(BF16)

    _trace_alternately([
        _mixer_tile(T, False, x_ref.at[g], h1_ref.at[g], w_refs,
                    (cos_ref, sin_ref, dmat_scr, qd_scr, kd_scr),
                    (c_scr.at[g], n_scr.at[g], m_scr.at[g], s_scr.at[g]),
                    qk_scr.at[g], mixed_scr.at[g]) for g in range(G)])


def _mixer_call(x3, meta_tokens, gmix, w_in_t, convw, bias_col, mlg, rtg, w_out, tables, T, G, cast_weights=()):
    B, L, _ = x3.shape
    n_t = L // T
    cos_t, sin_t = tables
    const2 = lambda b, t: (0, 0)
    meta_block = T // CHUNK - 1
    in_specs = [
        pl.BlockSpec((G, T, D_MODEL), lambda b, t: (b, t, 0)),
        pl.BlockSpec(gmix.shape, const2),
        pl.BlockSpec(memory_space=pl.ANY),
        pl.BlockSpec(convw.shape, const2),
        pl.BlockSpec(bias_col.shape, const2),
        pl.BlockSpec(mlg.shape, const2),
        pl.BlockSpec(rtg.shape, const2),
        pl.BlockSpec(memory_space=pl.ANY),
        pl.BlockSpec((T, LANES), lambda b, t: (t + 1, 0)),
        pl.BlockSpec((T, LANES), lambda b, t: (t + 1, 0)),
        pl.BlockSpec((CHUNK, LANES), lambda b, t: (meta_block, 0)),
        pl.BlockSpec((CHUNK, LANES), lambda b, t: (meta_block, 0)),
        pl.BlockSpec(meta_tokens.shape, const2),
    ]
    out_shape = [jax.ShapeDtypeStruct((B, L, D_MODEL), F32), jax.ShapeDtypeStruct((CHUNK, D_MODEL), F32)]
    out_specs = [pl.BlockSpec((G, T, D_MODEL), lambda b, t: (b, t, 0)),
                 pl.BlockSpec((CHUNK, D_MODEL), const2)]
    cast_map = lambda b, t: (jnp.where(b == 0, t, n_t - 1), 0)
    for wf in cast_weights:
        rows = wf.shape[0] // n_t
        assert rows * n_t == wf.shape[0] and rows % BF16_SUBLANES == 0
        in_specs.append(pl.BlockSpec((rows, wf.shape[1]), cast_map))
        out_shape.append(jax.ShapeDtypeStruct(wf.shape, BF16))
        out_specs.append(pl.BlockSpec((rows, wf.shape[1]), cast_map))
    state = [(HEADS, DQK, DV), (SUBLANES, LANES), (SUBLANES, LANES), (HEADS, DQK, DV)]
    tile_scratch = lambda rows: [
        pltpu.VMEM((HEADS, rows, rows), F32),
        pltpu.VMEM((HEADS, rows, LANES), F32),
        pltpu.VMEM((HEADS, rows, LANES), F32),
    ]
    scratch = (
        [pltpu.VMEM((G,) + shp, F32) for shp in state]
        + [pltpu.VMEM((G, T + SUBLANES, 2 * HEADS * DQK), F32)]
        + tile_scratch(T)
        + [pltpu.VMEM((G, T, D_MIX), BF16),
           pltpu.VMEM((D_MODEL, D_GROUP), BF16), pltpu.VMEM((D_MODEL, D_GROUP), BF16),
           pltpu.VMEM((D_MODEL, LANES), BF16), pltpu.VMEM((D_MIX, D_MODEL), BF16),
           pltpu.VMEM((2, WEIGHT_ROWS, D_MODEL), F32), pltpu.SemaphoreType.DMA((2,))]
        + [pltpu.VMEM(shp, F32) for shp in state]
        + [pltpu.VMEM((SUBLANES, 2 * HEADS * DQK), F32),
           pltpu.VMEM((CHUNK + SUBLANES, 2 * HEADS * DQK), F32)]
        + tile_scratch(CHUNK)
        + [pltpu.VMEM((CHUNK, D_MIX), BF16), pltpu.VMEM((CHUNK, D_MODEL), F32)])
    return pl.pallas_call(
        functools.partial(_mixer_kernel, T, G, len(cast_weights)),
        out_shape=tuple(out_shape),
        grid=(B // G, n_t),
        in_specs=in_specs,
        out_specs=tuple(out_specs),
        scratch_shapes=scratch,
        compiler_params=pltpu.CompilerParams(
            dimension_semantics=("arbitrary", "arbitrary"),
            vmem_limit_bytes=MIXER_VMEM_LIMIT_BYTES),
        name="mixer",
    )(x3, gmix, w_in_t, convw, bias_col, mlg, rtg, w_out, cos_t, sin_t, cos_t, sin_t, meta_tokens,
      *cast_weights)


def _ffn_tile(T, h1_ref, out_ref, w_refs, a_scr, z_scr):
    gffn_ref, wup_ref, wgate_ref, convw_ref, wdown_ref, gfin_ref = w_refs
    x1 = h1_ref[...]
    u = _rmsnorm(x1, gffn_ref[...]).astype(BF16)
    for c in range(D_FF // FFN_COLS):
        cols = slice(c * FFN_COLS, (c + 1) * FFN_COLS)
        a_scr[SUBLANES:SUBLANES + T, cols] = _dot(u, wup_ref[:, cols])
        a = convw_ref[FFN_CONV - 1:FFN_CONV, cols] * a_scr[SUBLANES:SUBLANES + T, cols]
        for k in range(FFN_CONV - 1):
            r0 = SUBLANES - (FFN_CONV - 1) + k
            a = a + convw_ref[k:k + 1, cols] * a_scr[r0:r0 + T, cols]
        gate = _dot(u, wgate_ref[:, cols])
        z_scr[:, cols] = (_silu(a) * gate).astype(BF16)
        yield
    a_scr[0:SUBLANES, :] = a_scr[T:T + SUBLANES, :]

    y = x1 + _dot(z_scr[...], wdown_ref[...])
    out_ref[...] = _rmsnorm(y, gfin_ref[...])


def _ffn_kernel(T, G, h1_ref, gffn_ref, wup_ref, wgate_ref, convw_ref, wdown_ref, gfin_ref, h1_meta_ref,
                out_ref, a_scr, z_scr, carry_scr):
    b = pl.program_id(0)
    t = pl.program_id(1)

    @pl.when((b == 0) & (t == 0))
    def _meta_carry():
        tail = h1_meta_ref[CHUNK - SUBLANES:CHUNK, :]
        carry_scr[...] = _dot(_rmsnorm(tail, gffn_ref[...]).astype(BF16), wup_ref[...])

    @pl.when(t == 0)
    def _load_state():
        for g in range(G):
            a_scr[g, 0:SUBLANES, :] = carry_scr[...]

    w_refs = (gffn_ref, wup_ref, wgate_ref, convw_ref, wdown_ref, gfin_ref)
    _trace_alternately([_ffn_tile(T, h1_ref.at[g], out_ref.at[g], w_refs, a_scr.at[g], z_scr.at[g])
                        for g in range(G)])


def _ffn_call(h1, weights, h1_meta, T, G):
    B, L, _ = h1.shape
    n_t = L // T
    gffn, wup, wgate, convw, wdown, gfin = weights
    const2 = lambda b, t: (0, 0)
    resident = dict(pipeline_mode=pl.Buffered(1))
    in_specs = [
        pl.BlockSpec((G, T, D_MODEL), lambda b, t: (b, t, 0)),
        pl.BlockSpec(gffn.shape, const2),
        pl.BlockSpec(wup.shape, const2, **resident),
        pl.BlockSpec(wgate.shape, const2, **resident),
        pl.BlockSpec((None,) + convw.shape[1:], lambda b, t: (0, 0, 0)),
        pl.BlockSpec(wdown.shape, const2, **resident),
        pl.BlockSpec(gfin.shape, const2),
        pl.BlockSpec(h1_meta.shape, const2),
    ]
    return pl.pallas_call(
        functools.partial(_ffn_kernel, T, G),
        out_shape=jax.ShapeDtypeStruct((B, L, D_MODEL), F32),
        grid=(B // G, n_t),
        in_specs=in_specs,
        out_specs=pl.BlockSpec((G, T, D_MODEL), lambda b, t: (b, t, 0)),
        scratch_shapes=[pltpu.VMEM((G, T + SUBLANES, D_FF), F32), pltpu.VMEM((G, T, D_FF), BF16),
                        pltpu.VMEM((SUBLANES, D_FF), F32)],
        compiler_params=pltpu.CompilerParams(
            dimension_semantics=("arbitrary", "arbitrary"),
            vmem_limit_bytes=VMEM_LIMIT_BYTES),
        name="ffn",
    )(h1, gffn, wup, wgate, convw, wdown, gfin, h1_meta)


def kernel(x, meta_tokens, norm_mix_g, w_in, ml_conv_w, ml_b_i, ml_b_f, ml_norm_g, rt_norm_g, w_out,
           norm_ffn_g, w_up, w_gate, ffn_conv_w, w_down, norm_final_g):
    B, L, D = x.shape
    assert D == D_MODEL and L % TILE_MIX == 0 and L % TILE_FFN == 0
    assert w_in.shape[0] == 1, "single-layer block"

    w_in_t = jnp.transpose(w_in[0])
    assert w_in_t.shape[0] == 2 * D_GROUP + 2 * HEADS
    bias_col = jnp.concatenate([ml_b_i[0], ml_b_f[0]])[:, None]
    gmix, mlg, rtg = norm_mix_g[0][None, :], ml_norm_g[0][None, :], rt_norm_g[0][None, :]

    tables = _rope_tables(CHUNK - TILE_MIX, L + TILE_MIX)

    G = BATCH_GROUP if B % BATCH_GROUP == 0 else 1
    h1, h1_meta, wup, wgate, wdown = _mixer_call(
        x, meta_tokens.astype(F32), gmix, w_in_t, ml_conv_w[0], bias_col, mlg, rtg, w_out[0], tables,
        TILE_MIX, G, cast_weights=(w_up[0], w_gate[0], w_down[0]))
    ffn_w = (norm_ffn_g[0][None, :], wup, wgate, ffn_conv_w, wdown, norm_final_g[None, :])
    return _ffn_call(h1, ffn_w, h1_meta, TILE_FFN, G)
```

```python
import functools
import math

import jax
import jax.numpy as jnp
from jax import lax
from jax.experimental import pallas as pl
from jax.experimental.pallas import tpu as pltpu

F32 = jnp.float32
BF16 = jnp.bfloat16

D_MODEL = 1024
N_META = 16
CHUNK = 64
N_PAD = CHUNK - N_META
EPS = 1e-6
NEG = -1e30
HEADS = 4
DQK = 128
DV = 256
ML_CONV = 4
GATE_CAP = 15.0
ROPE_BASE = 10000.0
D_MIX = 2 * HEADS * DV
D_FF = 2816
FFN_CONV = 3
LANES = 128
SUBLANES = 8
BF16_SUBLANES = 16

OFF_Q = 0
OFF_K = OFF_Q + HEADS * DQK
OFF_V = OFF_K + HEADS * DQK
OFF_G = OFF_V + HEADS * DV
D_GROUP = OFF_G + HEADS * DV

TILE_MIX = 256
TILE_FFN = 256
BATCH_GROUP = 2
FFN_BATCH_GROUP = 4
FFN_COLS = 256
VMEM_LIMIT_BYTES = 56 * 1024 * 1024
PROLOGUE_VMEM_LIMIT_BYTES = 60 * 1024 * 1024

LOG_GAMMA = tuple(math.log1p(-(2.0 ** -(5.0 + h))) for h in range(HEADS))


def _dot(a, b):
    return jnp.dot(a, b, preferred_element_type=F32)


def _dot_nt(a, b):
    return lax.dot_general(a, b, (((1,), (1,)), ((), ())), preferred_element_type=F32)


def _sigmoid(x):
    return 0.5 + 0.5 * jnp.tanh(0.5 * x)


def _silu(x):
    hx = 0.5 * x
    return hx + hx * jnp.tanh(hx)


def _rmsnorm(x, g):
    ms = jnp.mean(x * x, axis=-1, keepdims=True)
    return x * lax.rsqrt(ms + EPS) * g


def _head_norm(h, g):
    mu = jnp.mean(h, axis=-1, keepdims=True)
    c = h - mu
    var = jnp.mean(c * c, axis=-1, keepdims=True)
    return c * lax.rsqrt(var + EPS) * g


def _cumsum_lanes_mxu(x):
    r, n = x.shape
    hi = x.astype(BF16).astype(F32)
    r1 = x - hi
    mid = r1.astype(BF16).astype(F32)
    lo = (r1 - mid).astype(BF16).astype(F32)
    jsrc = lax.broadcasted_iota(jnp.int32, (n, n), 0)
    jdst = lax.broadcasted_iota(jnp.int32, (n, n), 1)
    tri = jnp.where(jsrc <= jdst, 1.0, 0.0).astype(BF16)
    p = _dot(jnp.concatenate([hi, mid, lo], axis=0).astype(BF16), tri)
    return p[0:r] + p[r:2 * r] + p[2 * r:3 * r]


def _trace_alternately(stage_generators):
    live = list(stage_generators)
    while live:
        still = []
        for gen in live:
            try:
                next(gen)
                still.append(gen)
            except StopIteration:
                pass
        live = still


def _rope_kernel(pos0, n_blocks, cos_ref, sin_ref):
    half = DQK // 2
    lane1 = lax.broadcasted_iota(jnp.int32, (1, LANES), 1)
    pair = jnp.where(lane1 < half, lane1, lane1 - half).astype(F32)
    invf = jnp.exp(pair * (-2.0 * math.log(ROPE_BASE) / DQK))
    lane = lax.broadcasted_iota(jnp.int32, (CHUNK, LANES), 1)
    off = lax.broadcasted_iota(jnp.int32, (CHUNK, LANES), 0).astype(F32) * invf
    cos_o, sin_o = jnp.cos(off), jnp.sin(off)
    base = ((lax.broadcasted_iota(jnp.int32, (n_blocks, LANES), 0) * CHUNK + pos0).astype(F32) * invf)
    cos_b, sin_b = jnp.cos(base), jnp.sin(base)
    for a in range(n_blocks):
        cb = cos_b[a:a + 1, :]
        sb = sin_b[a:a + 1, :]
        rows = slice(a * CHUNK, (a + 1) * CHUNK)
        cos_ref[rows, :] = cb * cos_o - sb * sin_o
        s = sb * cos_o + cb * sin_o
        sin_ref[rows, :] = jnp.where(lane < half, -s, s)


def _rope_tables(pos0, n_rows):
    assert pos0 % CHUNK == 0 and n_rows % CHUNK == 0
    out = jax.ShapeDtypeStruct((n_rows, LANES), F32)
    return pl.pallas_call(
        functools.partial(_rope_kernel, pos0, n_rows // CHUNK),
        out_shape=(out, out),
        name="rope_tables",
    )()


def _mixer_tile(T, has_pad, x_ref, h1_ref, w_refs, tab_refs, state_refs, qk_scr, mixed_scr):
    gmix_ref, wml_ref, wrt_ref, wg_ref, convw_ref, bias_ref, mlg_ref, rtg_ref, wout_ref = w_refs
    cos_ref, sin_ref, dmat_scr, qd_scr, kd_scr = tab_refs
    c_scr, n_scr, m_scr, s_scr = state_refs

    x = x_ref[...]
    u = _rmsnorm(x, gmix_ref[...]).astype(BF16)

    qk_scr[SUBLANES:SUBLANES + T, :] = _dot(u, wml_ref[:, OFF_Q:OFF_V])

    def conv_silu(c0):
        cols = slice(c0, c0 + DQK)
        acc = convw_ref[ML_CONV - 1:ML_CONV, cols] * qk_scr[SUBLANES:SUBLANES + T, cols]
        for k in range(ML_CONV - 1):
            r0 = SUBLANES - (ML_CONV - 1) + k
            acc = acc + convw_ref[k:k + 1, cols] * qk_scr[r0:r0 + T, cols]
        return _silu(acc)

    g_pre = _dot(u, wg_ref[...])
    g8 = g_pre.T[0:SUBLANES, :] + bias_ref[...]
    row8 = lax.broadcasted_iota(jnp.int32, (SUBLANES, T), 0)
    lane8 = lax.broadcasted_iota(jnp.int32, (SUBLANES, T), 1)
    li8 = GATE_CAP * jnp.tanh(g8 / GATE_CAP)
    if has_pad:
        li8 = jnp.where(lane8 >= N_PAD, li8, NEG)
    lf8 = jnp.minimum(g8, 0.0) - jnp.log1p(jnp.exp(-jnp.abs(g8)))
    bc8 = _cumsum_lanes_mxu(jnp.where(row8 >= HEADS, lf8, 0.0))
    r8 = jnp.where(row8 < HEADS, li8 - pltpu.roll(bc8, HEADS, 0), bc8)
    col = jnp.concatenate([r8, jnp.zeros((LANES - SUBLANES, T), F32)], axis=0).T
    be8 = jnp.sum(jnp.where(lane8 == T - 1, bc8, 0.0), axis=1, keepdims=True)

    ii = lax.broadcasted_iota(jnp.int32, (T, T), 0)
    jj = lax.broadcasted_iota(jnp.int32, (T, T), 1)
    causal = jj <= ii
    if has_pad:
        valid_col = (lax.broadcasted_iota(jnp.int32, (T, 1), 0) >= N_PAD).astype(F32)

    yield
    rq_all = _dot(u, wrt_ref[:, OFF_Q:OFF_K])
    rk_all = _dot(u, wrt_ref[:, OFF_K:OFF_V])
    cos_t = cos_ref[...]
    sin_t = sin_ref[...]
    for h in range(HEADS):
        hs = slice(h * DQK, (h + 1) * DQK)
        tq = rq_all[:, hs]
        tk = rk_all[:, hs]
        rq = tq * cos_t + pltpu.roll(tq, DQK // 2, 1) * sin_t
        rk = (tk * cos_t + pltpu.roll(tk, DQK // 2, 1) * sin_t) * (DQK ** -0.5)
        if has_pad:
            rk = rk * valid_col
        rv = _dot(u, wrt_ref[:, OFF_V + h * DV:OFF_V + (h + 1) * DV]).astype(BF16)
        yield
        scores = _dot_nt(rq.astype(BF16), rk.astype(BF16)) * dmat_scr[h]
        hr = (_dot(scores.astype(BF16), rv)
              + _dot((rq * qd_scr[h]).astype(BF16), s_scr[h].astype(BF16)))
        s_scr[h] = (math.exp(LOG_GAMMA[h] * T) * s_scr[h]
                    + _dot((rk * kd_scr[h]).T.astype(BF16), rv))
        yield
        vs = slice(h * DV, (h + 1) * DV)
        g_gate = _dot(u, wrt_ref[:, OFF_G + h * DV:OFF_G + (h + 1) * DV])
        y = _head_norm(hr, rtg_ref[:, vs]) * _silu(g_gate)
        mixed_scr[:, HEADS * DV + h * DV:HEADS * DV + (h + 1) * DV] = y.astype(BF16)
        yield

    for h in range(HEADS):
        q = conv_silu(OFF_Q + h * DQK)
        k = conv_silu(OFF_K + h * DQK) * (DQK ** -0.5)
        v = _dot(u, wml_ref[:, OFF_V + h * DV:OFF_V + (h + 1) * DV]).astype(BF16)
        qb = q.astype(BF16)
        yield
        a_col = col[:, h:h + 1]
        b_col = col[:, HEADS + h:HEADS + h + 1]
        a_row = r8[h:h + 1, :]
        m_h = m_scr[HEADS + h:HEADS + h + 1, 0:1]
        be_h = be8[HEADS + h:HEADS + h + 1, :]

        log_d = jnp.where(causal, b_col + a_row, -jnp.inf)
        log_inter = b_col + m_h
        m_row = jnp.maximum(log_inter, jnp.max(log_d, axis=1, keepdims=True))
        w_intra = jnp.exp(log_d - m_row)
        w_inter = jnp.exp(log_inter - m_row)
        s = _dot_nt(qb, k.astype(BF16)) * w_intra
        n_h = n_scr[h:h + 1, :]
        den = (jnp.sum(s, axis=1, keepdims=True)
               + w_inter * jnp.sum(q * n_h, axis=1, keepdims=True))
        num = _dot(s.astype(BF16), v) + w_inter * _dot(qb, c_scr[h].astype(BF16))
        hh = num / jnp.maximum(jnp.abs(den), jnp.exp(-m_row))

        yield
        lte = be_h + a_col
        m_new = jnp.maximum(be_h + m_h, jnp.max(lte, axis=0, keepdims=True))
        w_src = jnp.exp(lte - m_new)
        decay = jnp.exp(be_h + m_h - m_new)
        kw = k * w_src
        c_scr[h] = decay * c_scr[h] + _dot(kw.T.astype(BF16), v)
        n_scr[h:h + 1, :] = decay * n_h + jnp.sum(kw, axis=0, keepdims=True)
        m_scr[HEADS + h:HEADS + h + 1, :] = jnp.broadcast_to(m_new, (1, LANES))
        yield

        vs = slice(h * DV, (h + 1) * DV)
        o_gate = _dot(u, wml_ref[:, OFF_G + h * DV:OFF_G + (h + 1) * DV])
        y = _head_norm(hh, mlg_ref[:, vs]) * _sigmoid(o_gate)
        mixed_scr[:, vs] = y.astype(BF16)
        yield

    qk_scr[0:SUBLANES, :] = qk_scr[T:T + SUBLANES, :]

    h1_ref[...] = x + _dot(mixed_scr[...], wout_ref[...])


def _build_decay_tables(T, dmat_scr, qd_scr, kd_scr):
    ii = lax.broadcasted_iota(jnp.int32, (T, T), 0)
    jj = lax.broadcasted_iota(jnp.int32, (T, T), 1)
    rel = (ii - jj).astype(F32)
    ri = lax.broadcasted_iota(jnp.int32, (T, LANES), 0).astype(F32)
    for h in range(HEADS):
        lg = LOG_GAMMA[h]
        dmat_scr[h] = jnp.where(rel >= 0.0, jnp.exp(lg * jnp.maximum(rel, 0.0)), 0.0)
        qd_scr[h] = jnp.exp(lg * (ri + 1.0))
        kd_scr[h] = jnp.exp(lg * (T - 1.0 - ri))


def _mixer_kernel(T, G, n_cast, *refs):
    x_ref = refs[0]
    w_refs = refs[1:10]
    cos_ref, sin_ref, c0_ref, n0_ref, m0_ref, s0_ref, cc0_ref = refs[10:17]
    cast_in = refs[17:17 + n_cast]
    h1_ref = refs[17 + n_cast]
    cast_out = refs[18 + n_cast:18 + 2 * n_cast]
    c_scr, n_scr, m_scr, s_scr, qk_scr, dmat_scr, qd_scr, kd_scr, mixed_scr = refs[18 + 2 * n_cast:]

    b = pl.program_id(0)
    t = pl.program_id(1)

    @pl.when(t == 0)
    def _load_state():
        for g in range(G):
            c_scr[g] = c0_ref[...]
            n_scr[g] = n0_ref[...]
            m_scr[g] = m0_ref[...]
            s_scr[g] = s0_ref[...]
            qk_scr[g, 0:SUBLANES, :] = cc0_ref[...]

    @pl.when((b == 0) & (t == 0))
    def _tables():
        _build_decay_tables(T, dmat_scr, qd_scr, kd_scr)

    for src, dst in zip(cast_in, cast_out):
        dst[...] = src[...].astype(BF16)

    _trace_alternately([
        _mixer_tile(T, False, x_ref.at[g], h1_ref.at[g], w_refs,
                    (cos_ref, sin_ref, dmat_scr, qd_scr, kd_scr),
                    (c_scr.at[g], n_scr.at[g], m_scr.at[g], s_scr.at[g]),
                    qk_scr.at[g], mixed_scr.at[g]) for g in range(G)])


def _mixer_prologue_kernel(x_ref, gmix_ref, w_in_t_ref, convw_ref, bias_ref, mlg_ref, rtg_ref,
                           w_out_ref, cos_ref, sin_ref,
                           h1_ref, wml_ref, wrt_ref, wg_ref, wout_ref, c_ref, n_ref, m_ref, s_ref, cc_ref,
                           qk_scr, dmat_scr, qd_scr, kd_scr, mixed_scr, x_scr):
    x_scr[...] = jnp.concatenate([jnp.zeros((N_PAD, D_MODEL), F32), x_ref[...]], axis=0)
    blk = DV
    for c in range(D_GROUP // blk):
        cols = slice(c * blk, (c + 1) * blk)
        wml_ref[:, cols] = w_in_t_ref[c * blk:(c + 1) * blk, :].T.astype(BF16)
        r0 = D_GROUP + 2 * HEADS + c * blk
        wrt_ref[:, cols] = w_in_t_ref[r0:r0 + blk, :].T.astype(BF16)
    gates_t = jnp.concatenate([w_in_t_ref[D_GROUP:D_GROUP + 2 * HEADS, :],
                               jnp.zeros((LANES - 2 * HEADS, D_MODEL), F32)], axis=0)
    wg_ref[...] = gates_t.T.astype(BF16)
    wout_ref[...] = w_out_ref[...].astype(BF16)

    c_ref[...] = jnp.zeros(c_ref.shape, F32)
    n_ref[...] = jnp.zeros(n_ref.shape, F32)
    m_ref[...] = jnp.full(m_ref.shape, NEG, F32)
    s_ref[...] = jnp.zeros(s_ref.shape, F32)
    qk_scr[0:SUBLANES, :] = jnp.zeros((SUBLANES, qk_scr.shape[1]), F32)
    _build_decay_tables(CHUNK, dmat_scr, qd_scr, kd_scr)

    w_refs = (gmix_ref, wml_ref, wrt_ref, wg_ref, convw_ref, bias_ref, mlg_ref, rtg_ref, wout_ref)
    _trace_alternately([
        _mixer_tile(CHUNK, True, x_scr, h1_ref, w_refs, (cos_ref, sin_ref, dmat_scr, qd_scr, kd_scr),
                    (c_ref, n_ref, m_ref, s_ref), qk_scr, mixed_scr)])
    cc_ref[...] = qk_scr[0:SUBLANES, :]


def _mixer_prologue_call(meta_tokens, gmix, w_in_t, convw, bias_col, mlg, rtg, w_out, tables, table_block):
    cos_t, sin_t = tables
    T = CHUNK
    ins = (meta_tokens, gmix, w_in_t, convw, bias_col, mlg, rtg, w_out)
    whole = lambda a: pl.BlockSpec(a.shape, lambda i, nd=a.ndim: (0,) * nd)
    in_specs = [whole(a) for a in ins] + [pl.BlockSpec((T, LANES), lambda i: (table_block, 0))] * 2
    f32 = lambda *shape: jax.ShapeDtypeStruct(shape, F32)
    bf16 = lambda *shape: jax.ShapeDtypeStruct(shape, BF16)
    out_shape = (f32(T, D_MODEL), bf16(D_MODEL, D_GROUP), bf16(D_MODEL, D_GROUP), bf16(D_MODEL, LANES),
                 bf16(D_MIX, D_MODEL), f32(HEADS, DQK, DV), f32(SUBLANES, LANES), f32(SUBLANES, LANES),
                 f32(HEADS, DQK, DV), f32(SUBLANES, 2 * HEADS * DQK))
    scratch = [
        pltpu.VMEM((T + SUBLANES, 2 * HEADS * DQK), F32),
        pltpu.VMEM((HEADS, T, T), F32),
        pltpu.VMEM((HEADS, T, LANES), F32),
        pltpu.VMEM((HEADS, T, LANES), F32),
        pltpu.VMEM((T, D_MIX), BF16),
        pltpu.VMEM((T, D_MODEL), F32),
    ]
    return pl.pallas_call(
        _mixer_prologue_kernel,
        out_shape=out_shape,
        grid=(1,),
        in_specs=in_specs,
        out_specs=tuple(pl.BlockSpec(o.shape, lambda i, nd=len(o.shape): (0,) * nd) for o in out_shape),
        scratch_shapes=scratch,
        compiler_params=pltpu.CompilerParams(dimension_semantics=("arbitrary",),
                                             vmem_limit_bytes=PROLOGUE_VMEM_LIMIT_BYTES),
        name="mixer_prologue",
    )(*ins, cos_t, sin_t)


def _mixer_call(x3, weights, tables, state, T, G, cast_weights=()):
    B, L, _ = x3.shape
    n_t = L // T
    gmix, wml, wrt, wg, convw, bias_col, mlg, rtg, wout = weights
    cos_t, sin_t = tables
    c0, n0, m0, s0, cc0 = state
    const2 = lambda b, t: (0, 0)
    const3 = lambda b, t: (0, 0, 0)
    in_specs = [
        pl.BlockSpec((G, T, D_MODEL), lambda b, t: (b, t, 0)),
        pl.BlockSpec(gmix.shape, const2),
        pl.BlockSpec(wml.shape, const2),
        pl.BlockSpec(wrt.shape, const2),
        pl.BlockSpec(wg.shape, const2),
        pl.BlockSpec(convw.shape, const2),
        pl.BlockSpec(bias_col.shape, const2),
        pl.BlockSpec(mlg.shape, const2),
        pl.BlockSpec(rtg.shape, const2),
        pl.BlockSpec(wout.shape, const2),
        pl.BlockSpec((T, LANES), lambda b, t: (t + 1, 0)),
        pl.BlockSpec((T, LANES), lambda b, t: (t + 1, 0)),
        pl.BlockSpec(c0.shape, const3),
        pl.BlockSpec(n0.shape, const2),
        pl.BlockSpec(m0.shape, const2),
        pl.BlockSpec(s0.shape, const3),
        pl.BlockSpec(cc0.shape, const2),
    ]
    out_shape = [jax.ShapeDtypeStruct((B, L, D_MODEL), F32)]
    out_specs = [pl.BlockSpec((G, T, D_MODEL), lambda b, t: (b, t, 0))]
    cast_map = lambda b, t: (jnp.where(b == 0, t, n_t - 1), 0)
    for wf in cast_weights:
        rows = wf.shape[0] // n_t
        assert rows * n_t == wf.shape[0] and rows % BF16_SUBLANES == 0
        in_specs.append(pl.BlockSpec((rows, wf.shape[1]), cast_map))
        out_shape.append(jax.ShapeDtypeStruct(wf.shape, BF16))
        out_specs.append(pl.BlockSpec((rows, wf.shape[1]), cast_map))
    scratch = [
        pltpu.VMEM((G,) + c0.shape, F32),
        pltpu.VMEM((G,) + n0.shape, F32),
        pltpu.VMEM((G,) + m0.shape, F32),
        pltpu.VMEM((G,) + s0.shape, F32),
        pltpu.VMEM((G, T + SUBLANES, 2 * HEADS * DQK), F32),
        pltpu.VMEM((HEADS, T, T), F32),
        pltpu.VMEM((HEADS, T, LANES), F32),
        pltpu.VMEM((HEADS, T, LANES), F32),
        pltpu.VMEM((G, T, D_MIX), BF16),
    ]
    return pl.pallas_call(
        functools.partial(_mixer_kernel, T, G, len(cast_weights)),
        out_shape=tuple(out_shape),
        grid=(B // G, n_t),
        in_specs=in_specs,
        out_specs=tuple(out_specs),
        scratch_shapes=scratch,
        compiler_params=pltpu.CompilerParams(
            dimension_semantics=("arbitrary", "arbitrary"),
            vmem_limit_bytes=VMEM_LIMIT_BYTES),
        name="mixer",
    )(x3, gmix, wml, wrt, wg, convw, bias_col, mlg, rtg, wout, cos_t, sin_t, c0, n0, m0, s0, cc0,
      *cast_weights)


def _ffn_tile(T, h1_ref, out_ref, w_refs, a_scr, z_scr):
    gffn_ref, wup_ref, wgate_ref, convw_ref, wdown_ref, gfin_ref = w_refs
    x1 = h1_ref[...]
    u = _rmsnorm(x1, gffn_ref[...]).astype(BF16)
    for c in range(D_FF // FFN_COLS):
        cols = slice(c * FFN_COLS, (c + 1) * FFN_COLS)
        a_scr[SUBLANES:SUBLANES + T, cols] = _dot(u, wup_ref[:, cols])
        a = convw_ref[FFN_CONV - 1:FFN_CONV, cols] * a_scr[SUBLANES:SUBLANES + T, cols]
        for k in range(FFN_CONV - 1):
            r0 = SUBLANES - (FFN_CONV - 1) + k
            a = a + convw_ref[k:k + 1, cols] * a_scr[r0:r0 + T, cols]
        gate = _dot(u, wgate_ref[:, cols])
        z_scr[:, cols] = (_silu(a) * gate).astype(BF16)
        yield
    a_scr[0:SUBLANES, :] = a_scr[T:T + SUBLANES, :]

    y = x1 + _dot(z_scr[...], wdown_ref[...])
    out_ref[...] = _rmsnorm(y, gfin_ref[...])


def _ffn_kernel(T, G, h1_ref, gffn_ref, wup_ref, wgate_ref, convw_ref, wdown_ref, gfin_ref, h1_meta_ref,
                out_ref, a_scr, z_scr, carry_scr):
    b = pl.program_id(0)
    t = pl.program_id(1)

    @pl.when((b == 0) & (t == 0))
    def _meta_carry():
        tail = h1_meta_ref[CHUNK - SUBLANES:CHUNK, :]
        carry_scr[...] = _dot(_rmsnorm(tail, gffn_ref[...]).astype(BF16), wup_ref[...])

    @pl.when(t == 0)
    def _load_state():
        for g in range(G):
            a_scr[g, 0:SUBLANES, :] = carry_scr[...]

    w_refs = (gffn_ref, wup_ref, wgate_ref, convw_ref, wdown_ref, gfin_ref)
    _trace_alternately([_ffn_tile(T, h1_ref.at[g], out_ref.at[g], w_refs, a_scr.at[g], z_scr.at[g])
                        for g in range(G)])


def _ffn_call(h1, weights, h1_meta, T, G):
    B, L, _ = h1.shape
    n_t = L // T
    gffn, wup, wgate, convw, wdown, gfin = weights
    const2 = lambda b, t: (0, 0)
    resident = dict(pipeline_mode=pl.Buffered(1))
    in_specs = [
        pl.BlockSpec((G, T, D_MODEL), lambda b, t: (b, t, 0)),
        pl.BlockSpec(gffn.shape, const2),
        pl.BlockSpec(wup.shape, const2, **resident),
        pl.BlockSpec(wgate.shape, const2, **resident),
        pl.BlockSpec((None,) + convw.shape[1:], lambda b, t: (0, 0, 0)),
        pl.BlockSpec(wdown.shape, const2, **resident),
        pl.BlockSpec(gfin.shape, const2),
        pl.BlockSpec(h1_meta.shape, const2),
    ]
    return pl.pallas_call(
        functools.partial(_ffn_kernel, T, G),
        out_shape=jax.ShapeDtypeStruct((B, L, D_MODEL), F32),
        grid=(B // G, n_t),
        in_specs=in_specs,
        out_specs=pl.BlockSpec((G, T, D_MODEL), lambda b, t: (b, t, 0)),
        scratch_shapes=[pltpu.VMEM((G, T + SUBLANES, D_FF), F32), pltpu.VMEM((G, T, D_FF), BF16),
                        pltpu.VMEM((SUBLANES, D_FF), F32)],
        compiler_params=pltpu.CompilerParams(
            dimension_semantics=("arbitrary", "arbitrary"),
            vmem_limit_bytes=VMEM_LIMIT_BYTES),
        name="ffn",
    )(h1, gffn, wup, wgate, convw, wdown, gfin, h1_meta)


def kernel(x, meta_tokens, norm_mix_g, w_in, ml_conv_w, ml_b_i, ml_b_f, ml_norm_g, rt_norm_g, w_out,
           norm_ffn_g, w_up, w_gate, ffn_conv_w, w_down, norm_final_g):
    B, L, D = x.shape
    assert D == D_MODEL and L % TILE_MIX == 0 and L % TILE_FFN == 0
    assert w_in.shape[0] == 1, "single-layer block"

    w_in_t = jnp.transpose(w_in[0])
    assert w_in_t.shape[0] == 2 * D_GROUP + 2 * HEADS
    bias_col = jnp.concatenate([ml_b_i[0], ml_b_f[0]])[:, None]
    gmix, mlg, rtg = norm_mix_g[0][None, :], ml_norm_g[0][None, :], rt_norm_g[0][None, :]

    tables = _rope_tables(CHUNK - TILE_MIX, L + TILE_MIX)

    h1_meta, wml, wrt, wg, wout, *state = _mixer_prologue_call(
        meta_tokens.astype(F32), gmix, w_in_t, ml_conv_w[0], bias_col, mlg, rtg, w_out[0], tables,
        table_block=TILE_MIX // CHUNK - 1)
    mixer_w = (gmix, wml, wrt, wg, ml_conv_w[0], bias_col, mlg, rtg, wout)

    G = BATCH_GROUP if B % BATCH_GROUP == 0 else 1
    h1, wup, wgate, wdown = _mixer_call(x, mixer_w, tables, tuple(state), TILE_MIX, G,
                                        cast_weights=(w_up[0], w_gate[0], w_down[0]))
    ffn_w = (norm_ffn_g[0][None, :], wup, wgate, ffn_conv_w, wdown, norm_final_g[None, :])
    G_ffn = FFN_BATCH_GROUP if B % FFN_BATCH_GROUP == 0 else G
    return _ffn_call(h1, ffn_w, h1_meta, TILE_FFN, G_ffn)
```

```python
import functools
import math

import jax
import jax.numpy as jnp
from jax import lax
from jax.experimental import pallas as pl
from jax.experimental.pallas import tpu as pltpu

F32 = jnp.float32
BF16 = jnp.bfloat16

D_MODEL = 1024
N_META = 16
CHUNK = 64
N_PAD = CHUNK - N_META
EPS = 1e-6
NEG = -1e30
HEADS = 4
DQK = 128
DV = 256
ML_CONV = 4
GATE_CAP = 15.0
ROPE_BASE = 10000.0
D_MIX = 2 * HEADS * DV
D_FF = 2816
FFN_CONV = 3
LANES = 128
SUBLANES = 8
BF16_SUBLANES = 16

OFF_Q = 0
OFF_K = OFF_Q + HEADS * DQK
OFF_V = OFF_K + HEADS * DQK
OFF_G = OFF_V + HEADS * DV
D_GROUP = OFF_G + HEADS * DV

TILE_MIX = 256
TILE_FFN = 256
BATCH_GROUP = 2
FFN_COLS = 256
VMEM_LIMIT_BYTES = 56 * 1024 * 1024
PROLOGUE_VMEM_LIMIT_BYTES = 60 * 1024 * 1024

LOG_GAMMA = tuple(math.log1p(-(2.0 ** -(5.0 + h))) for h in range(HEADS))


def _dot(a, b):
    return jnp.dot(a, b, preferred_element_type=F32)


def _dot_nt(a, b):
    return lax.dot_general(a, b, (((1,), (1,)), ((), ())), preferred_element_type=F32)


def _sigmoid(x):
    return 0.5 + 0.5 * jnp.tanh(0.5 * x)


def _silu(x):
    hx = 0.5 * x
    return hx + hx * jnp.tanh(hx)


def _rmsnorm(x, g):
    ms = jnp.mean(x * x, axis=-1, keepdims=True)
    return x * lax.rsqrt(ms + EPS) * g


def _head_norm(h, g):
    mu = jnp.mean(h, axis=-1, keepdims=True)
    c = h - mu
    var = jnp.mean(c * c, axis=-1, keepdims=True)
    return c * lax.rsqrt(var + EPS) * g


def _cumsum_lanes_mxu(x):
    r, n = x.shape
    hi = x.astype(BF16).astype(F32)
    r1 = x - hi
    mid = r1.astype(BF16).astype(F32)
    lo = (r1 - mid).astype(BF16).astype(F32)
    jsrc = lax.broadcasted_iota(jnp.int32, (n, n), 0)
    jdst = lax.broadcasted_iota(jnp.int32, (n, n), 1)
    tri = jnp.where(jsrc <= jdst, 1.0, 0.0).astype(BF16)
    p = _dot(jnp.concatenate([hi, mid, lo], axis=0).astype(BF16), tri)
    return p[0:r] + p[r:2 * r] + p[2 * r:3 * r]


def _trace_alternately(stage_generators):
    live = list(stage_generators)
    while live:
        still = []
        for gen in live:
            try:
                next(gen)
                still.append(gen)
            except StopIteration:
                pass
        live = still


def _rope_kernel(pos0, n_blocks, cos_ref, sin_ref):
    half = DQK // 2
    lane1 = lax.broadcasted_iota(jnp.int32, (1, LANES), 1)
    pair = jnp.where(lane1 < half, lane1, lane1 - half).astype(F32)
    invf = jnp.exp(pair * (-2.0 * math.log(ROPE_BASE) / DQK))
    lane = lax.broadcasted_iota(jnp.int32, (CHUNK, LANES), 1)
    off = lax.broadcasted_iota(jnp.int32, (CHUNK, LANES), 0).astype(F32) * invf
    cos_o, sin_o = jnp.cos(off), jnp.sin(off)
    base = ((lax.broadcasted_iota(jnp.int32, (n_blocks, LANES), 0) * CHUNK + pos0).astype(F32) * invf)
    cos_b, sin_b = jnp.cos(base), jnp.sin(base)
    for a in range(n_blocks):
        cb = cos_b[a:a + 1, :]
        sb = sin_b[a:a + 1, :]
        rows = slice(a * CHUNK, (a + 1) * CHUNK)
        cos_ref[rows, :] = cb * cos_o - sb * sin_o
        s = sb * cos_o + cb * sin_o
        sin_ref[rows, :] = jnp.where(lane < half, -s, s)


def _rope_tables(pos0, n_rows):
    assert pos0 % CHUNK == 0 and n_rows % CHUNK == 0
    out = jax.ShapeDtypeStruct((n_rows, LANES), F32)
    return pl.pallas_call(
        functools.partial(_rope_kernel, pos0, n_rows // CHUNK),
        out_shape=(out, out),
        name="rope_tables",
    )()


def _mixer_tile(T, has_pad, x_ref, h1_ref, w_refs, tab_refs, state_refs, qk_scr, mixed_scr):
    gmix_ref, wml_ref, wrt_ref, wg_ref, convw_ref, bias_ref, mlg_ref, rtg_ref, wout_ref = w_refs
    cos_ref, sin_ref, dmat_scr, qd_scr, kd_scr = tab_refs
    c_scr, n_scr, m_scr, s_scr = state_refs

    x = x_ref[...]
    u = _rmsnorm(x, gmix_ref[...]).astype(BF16)

    qk_scr[SUBLANES:SUBLANES + T, :] = _dot(u, wml_ref[:, OFF_Q:OFF_V])

    def conv_silu(c0):
        cols = slice(c0, c0 + DQK)
        acc = convw_ref[ML_CONV - 1:ML_CONV, cols] * qk_scr[SUBLANES:SUBLANES + T, cols]
        for k in range(ML_CONV - 1):
            r0 = SUBLANES - (ML_CONV - 1) + k
            acc = acc + convw_ref[k:k + 1, cols] * qk_scr[r0:r0 + T, cols]
        return _silu(acc)

    g_pre = _dot(u, wg_ref[...])
    g8 = g_pre.T[0:SUBLANES, :] + bias_ref[...]
    row8 = lax.broadcasted_iota(jnp.int32, (SUBLANES, T), 0)
    lane8 = lax.broadcasted_iota(jnp.int32, (SUBLANES, T), 1)
    li8 = GATE_CAP * jnp.tanh(g8 / GATE_CAP)
    if has_pad:
        li8 = jnp.where(lane8 >= N_PAD, li8, NEG)
    lf8 = jnp.minimum(g8, 0.0) - jnp.log1p(jnp.exp(-jnp.abs(g8)))
    bc8 = _cumsum_lanes_mxu(jnp.where(row8 >= HEADS, lf8, 0.0))
    r8 = jnp.where(row8 < HEADS, li8 - pltpu.roll(bc8, HEADS, 0), bc8)
    col = jnp.concatenate([r8, jnp.zeros((LANES - SUBLANES, T), F32)], axis=0).T
    be8 = jnp.sum(jnp.where(lane8 == T - 1, bc8, 0.0), axis=1, keepdims=True)

    ii = lax.broadcasted_iota(jnp.int32, (T, T), 0)
    jj = lax.broadcasted_iota(jnp.int32, (T, T), 1)
    causal = jj <= ii
    if has_pad:
        valid_col = (lax.broadcasted_iota(jnp.int32, (T, 1), 0) >= N_PAD).astype(F32)

    yield
    rq_all = _dot(u, wrt_ref[:, OFF_Q:OFF_K])
    rk_all = _dot(u, wrt_ref[:, OFF_K:OFF_V])
    cos_t = cos_ref[...]
    sin_t = sin_ref[...]
    for h in range(HEADS):
        hs = slice(h * DQK, (h + 1) * DQK)
        tq = rq_all[:, hs]
        tk = rk_all[:, hs]
        rq = tq * cos_t + pltpu.roll(tq, DQK // 2, 1) * sin_t
        rk = (tk * cos_t + pltpu.roll(tk, DQK // 2, 1) * sin_t) * (DQK ** -0.5)
        if has_pad:
            rk = rk * valid_col
        rv = _dot(u, wrt_ref[:, OFF_V + h * DV:OFF_V + (h + 1) * DV]).astype(BF16)
        yield
        scores = _dot_nt(rq.astype(BF16), rk.astype(BF16)) * dmat_scr[h]
        hr = (_dot(scores.astype(BF16), rv)
              + _dot((rq * qd_scr[h]).astype(BF16), s_scr[h].astype(BF16)))
        s_scr[h] = (math.exp(LOG_GAMMA[h] * T) * s_scr[h]
                    + _dot((rk * kd_scr[h]).T.astype(BF16), rv))
        yield
        vs = slice(h * DV, (h + 1) * DV)
        g_gate = _dot(u, wrt_ref[:, OFF_G + h * DV:OFF_G + (h + 1) * DV])
        y = _head_norm(hr, rtg_ref[:, vs]) * _silu(g_gate)
        mixed_scr[:, HEADS * DV + h * DV:HEADS * DV + (h + 1) * DV] = y.astype(BF16)
        yield

    for h in range(HEADS):
        q = conv_silu(OFF_Q + h * DQK)
        k = conv_silu(OFF_K + h * DQK) * (DQK ** -0.5)
        v = _dot(u, wml_ref[:, OFF_V + h * DV:OFF_V + (h + 1) * DV]).astype(BF16)
        qb = q.astype(BF16)
        yield
        a_col = col[:, h:h + 1]
        b_col = col[:, HEADS + h:HEADS + h + 1]
        a_row = r8[h:h + 1, :]
        m_h = m_scr[HEADS + h:HEADS + h + 1, 0:1]
        be_h = be8[HEADS + h:HEADS + h + 1, :]

        log_d = jnp.where(causal, b_col + a_row, -jnp.inf)
        log_inter = b_col + m_h
        m_row = jnp.maximum(log_inter, jnp.max(log_d, axis=1, keepdims=True))
        w_intra = jnp.exp(log_d - m_row)
        w_inter = jnp.exp(log_inter - m_row)
        s = _dot_nt(qb, k.astype(BF16)) * w_intra
        n_h = n_scr[h:h + 1, :]
        den = (jnp.sum(s, axis=1, keepdims=True)
               + w_inter * jnp.sum(q * n_h, axis=1, keepdims=True))
        num = _dot(s.astype(BF16), v) + w_inter * _dot(qb, c_scr[h].astype(BF16))
        hh = num / jnp.maximum(jnp.abs(den), jnp.exp(-m_row))

        yield
        lte = be_h + a_col
        m_new = jnp.maximum(be_h + m_h, jnp.max(lte, axis=0, keepdims=True))
        w_src = jnp.exp(lte - m_new)
        decay = jnp.exp(be_h + m_h - m_new)
        kw = k * w_src
        c_scr[h] = decay * c_scr[h] + _dot(kw.T.astype(BF16), v)
        n_scr[h:h + 1, :] = decay * n_h + jnp.sum(kw, axis=0, keepdims=True)
        m_scr[HEADS + h:HEADS + h + 1, :] = jnp.broadcast_to(m_new, (1, LANES))
        yield

        vs = slice(h * DV, (h + 1) * DV)
        o_gate = _dot(u, wml_ref[:, OFF_G + h * DV:OFF_G + (h + 1) * DV])
        y = _head_norm(hh, mlg_ref[:, vs]) * _sigmoid(o_gate)
        mixed_scr[:, vs] = y.astype(BF16)
        yield

    qk_scr[0:SUBLANES, :] = qk_scr[T:T + SUBLANES, :]

    if h1_ref is not None:
        h1_ref[...] = x + _dot(mixed_scr[...], wout_ref[...])


def _build_decay_tables(T, dmat_scr, qd_scr, kd_scr):
    ii = lax.broadcasted_iota(jnp.int32, (T, T), 0)
    jj = lax.broadcasted_iota(jnp.int32, (T, T), 1)
    rel = (ii - jj).astype(F32)
    ri = lax.broadcasted_iota(jnp.int32, (T, LANES), 0).astype(F32)
    for h in range(HEADS):
        lg = LOG_GAMMA[h]
        dmat_scr[h] = jnp.where(rel >= 0.0, jnp.exp(lg * jnp.maximum(rel, 0.0)), 0.0)
        qd_scr[h] = jnp.exp(lg * (ri + 1.0))
        kd_scr[h] = jnp.exp(lg * (T - 1.0 - ri))


def _mixer_kernel(T, G, n_cast, *refs):
    x_ref = refs[0]
    w_refs = refs[1:9] + (None,)
    cos_ref, sin_ref, c0_ref, n0_ref, m0_ref, s0_ref, cc0_ref = refs[9:16]
    cast_in = refs[16:16 + n_cast]
    mixed_ref = refs[16 + n_cast]
    cast_out = refs[17 + n_cast:17 + 2 * n_cast]
    c_scr, n_scr, m_scr, s_scr, qk_scr, dmat_scr, qd_scr, kd_scr = refs[17 + 2 * n_cast:]

    b = pl.program_id(0)
    t = pl.program_id(1)

    @pl.when(t == 0)
    def _load_state():
        for g in range(G):
            c_scr[g] = c0_ref[...]
            n_scr[g] = n0_ref[...]
            m_scr[g] = m0_ref[...]
            s_scr[g] = s0_ref[...]
            qk_scr[g, 0:SUBLANES, :] = cc0_ref[...]

    @pl.when((b == 0) & (t == 0))
    def _tables():
        _build_decay_tables(T, dmat_scr, qd_scr, kd_scr)

    for src, dst in zip(cast_in, cast_out):
        dst[...] = src[...].astype(BF16)

    _trace_alternately([
        _mixer_tile(T, False, x_ref.at[g], None, w_refs,
                    (cos_ref, sin_ref, dmat_scr, qd_scr, kd_scr),
                    (c_scr.at[g], n_scr.at[g], m_scr.at[g], s_scr.at[g]),
                    qk_scr.at[g], mixed_ref.at[g]) for g in range(G)])


def _mixer_prologue_kernel(x_ref, gmix_ref, w_in_t_ref, convw_ref, bias_ref, mlg_ref, rtg_ref,
                           w_out_ref, cos_ref, sin_ref,
                           h1_ref, wml_ref, wrt_ref, wg_ref, wout_ref, c_ref, n_ref, m_ref, s_ref, cc_ref,
                           qk_scr, dmat_scr, qd_scr, kd_scr, mixed_scr, x_scr):
    x_scr[...] = jnp.concatenate([jnp.zeros((N_PAD, D_MODEL), F32), x_ref[...]], axis=0)
    blk = DV
    for c in range(D_GROUP // blk):
        cols = slice(c * blk, (c + 1) * blk)
        wml_ref[:, cols] = w_in_t_ref[c * blk:(c + 1) * blk, :].T.astype(BF16)
        r0 = D_GROUP + 2 * HEADS + c * blk
        wrt_ref[:, cols] = w_in_t_ref[r0:r0 + blk, :].T.astype(BF16)
    gates_t = jnp.concatenate([w_in_t_ref[D_GROUP:D_GROUP + 2 * HEADS, :],
                               jnp.zeros((LANES - 2 * HEADS, D_MODEL), F32)], axis=0)
    wg_ref[...] = gates_t.T.astype(BF16)
    wout_ref[...] = w_out_ref[...].astype(BF16)

    c_ref[...] = jnp.zeros(c_ref.shape, F32)
    n_ref[...] = jnp.zeros(n_ref.shape, F32)
    m_ref[...] = jnp.full(m_ref.shape, NEG, F32)
    s_ref[...] = jnp.zeros(s_ref.shape, F32)
    qk_scr[0:SUBLANES, :] = jnp.zeros((SUBLANES, qk_scr.shape[1]), F32)
    _build_decay_tables(CHUNK, dmat_scr, qd_scr, kd_scr)

    w_refs = (gmix_ref, wml_ref, wrt_ref, wg_ref, convw_ref, bias_ref, mlg_ref, rtg_ref, wout_ref)
    _trace_alternately([
        _mixer_tile(CHUNK, True, x_scr, h1_ref, w_refs, (cos_ref, sin_ref, dmat_scr, qd_scr, kd_scr),
                    (c_ref, n_ref, m_ref, s_ref), qk_scr, mixed_scr)])
    cc_ref[...] = qk_scr[0:SUBLANES, :]


def _mixer_prologue_call(meta_tokens, gmix, w_in_t, convw, bias_col, mlg, rtg, w_out, tables, table_block):
    cos_t, sin_t = tables
    T = CHUNK
    ins = (meta_tokens, gmix, w_in_t, convw, bias_col, mlg, rtg, w_out)
    whole = lambda a: pl.BlockSpec(a.shape, lambda i, nd=a.ndim: (0,) * nd)
    in_specs = [whole(a) for a in ins] + [pl.BlockSpec((T, LANES), lambda i: (table_block, 0))] * 2
    f32 = lambda *shape: jax.ShapeDtypeStruct(shape, F32)
    bf16 = lambda *shape: jax.ShapeDtypeStruct(shape, BF16)
    out_shape = (f32(T, D_MODEL), bf16(D_MODEL, D_GROUP), bf16(D_MODEL, D_GROUP), bf16(D_MODEL, LANES),
                 bf16(D_MIX, D_MODEL), f32(HEADS, DQK, DV), f32(SUBLANES, LANES), f32(SUBLANES, LANES),
                 f32(HEADS, DQK, DV), f32(SUBLANES, 2 * HEADS * DQK))
    scratch = [
        pltpu.VMEM((T + SUBLANES, 2 * HEADS * DQK), F32),
        pltpu.VMEM((HEADS, T, T), F32),
        pltpu.VMEM((HEADS, T, LANES), F32),
        pltpu.VMEM((HEADS, T, LANES), F32),
        pltpu.VMEM((T, D_MIX), BF16),
        pltpu.VMEM((T, D_MODEL), F32),
    ]
    return pl.pallas_call(
        _mixer_prologue_kernel,
        out_shape=out_shape,
        grid=(1,),
        in_specs=in_specs,
        out_specs=tuple(pl.BlockSpec(o.shape, lambda i, nd=len(o.shape): (0,) * nd) for o in out_shape),
        scratch_shapes=scratch,
        compiler_params=pltpu.CompilerParams(dimension_semantics=("arbitrary",),
                                             vmem_limit_bytes=PROLOGUE_VMEM_LIMIT_BYTES),
        name="mixer_prologue",
    )(*ins, cos_t, sin_t)


def _mixer_call(x3, weights, tables, state, T, G, cast_weights=()):
    B, L, _ = x3.shape
    n_t = L // T
    gmix, wml, wrt, wg, convw, bias_col, mlg, rtg = weights
    cos_t, sin_t = tables
    c0, n0, m0, s0, cc0 = state
    const2 = lambda b, t: (0, 0)
    const3 = lambda b, t: (0, 0, 0)
    in_specs = [
        pl.BlockSpec((G, T, D_MODEL), lambda b, t: (b, t, 0)),
        pl.BlockSpec(gmix.shape, const2),
        pl.BlockSpec(wml.shape, const2),
        pl.BlockSpec(wrt.shape, const2),
        pl.BlockSpec(wg.shape, const2),
        pl.BlockSpec(convw.shape, const2),
        pl.BlockSpec(bias_col.shape, const2),
        pl.BlockSpec(mlg.shape, const2),
        pl.BlockSpec(rtg.shape, const2),
        pl.BlockSpec((T, LANES), lambda b, t: (t + 1, 0)),
        pl.BlockSpec((T, LANES), lambda b, t: (t + 1, 0)),
        pl.BlockSpec(c0.shape, const3),
        pl.BlockSpec(n0.shape, const2),
        pl.BlockSpec(m0.shape, const2),
        pl.BlockSpec(s0.shape, const3),
        pl.BlockSpec(cc0.shape, const2),
    ]
    out_shape = [jax.ShapeDtypeStruct((B, L, D_MIX), BF16)]
    out_specs = [pl.BlockSpec((G, T, D_MIX), lambda b, t: (b, t, 0))]
    cast_map = lambda b, t: (jnp.where(b == 0, t, n_t - 1), 0)
    for wf in cast_weights:
        rows = wf.shape[0] // n_t
        assert rows * n_t == wf.shape[0] and rows % BF16_SUBLANES == 0
        in_specs.append(pl.BlockSpec((rows, wf.shape[1]), cast_map))
        out_shape.append(jax.ShapeDtypeStruct(wf.shape, BF16))
        out_specs.append(pl.BlockSpec((rows, wf.shape[1]), cast_map))
    scratch = [
        pltpu.VMEM((G,) + c0.shape, F32),
        pltpu.VMEM((G,) + n0.shape, F32),
        pltpu.VMEM((G,) + m0.shape, F32),
        pltpu.VMEM((G,) + s0.shape, F32),
        pltpu.VMEM((G, T + SUBLANES, 2 * HEADS * DQK), F32),
        pltpu.VMEM((HEADS, T, T), F32),
        pltpu.VMEM((HEADS, T, LANES), F32),
        pltpu.VMEM((HEADS, T, LANES), F32),
    ]
    return pl.pallas_call(
        functools.partial(_mixer_kernel, T, G, len(cast_weights)),
        out_shape=tuple(out_shape),
        grid=(B // G, n_t),
        in_specs=in_specs,
        out_specs=tuple(out_specs),
        scratch_shapes=scratch,
        compiler_params=pltpu.CompilerParams(
            dimension_semantics=("arbitrary", "arbitrary"),
            vmem_limit_bytes=VMEM_LIMIT_BYTES),
        name="mixer",
    )(x3, gmix, wml, wrt, wg, convw, bias_col, mlg, rtg, cos_t, sin_t, c0, n0, m0, s0, cc0,
      *cast_weights)


def _ffn_tile(T, x_ref, mixed_ref, out_ref, w_refs, a_scr, z_scr):
    wout_ref, gffn_ref, wup_ref, wgate_ref, convw_ref, wdown_ref, gfin_ref = w_refs
    x1 = x_ref[...] + _dot(mixed_ref[...], wout_ref[...])
    u = _rmsnorm(x1, gffn_ref[...]).astype(BF16)
    for c in range(D_FF // FFN_COLS):
        cols = slice(c * FFN_COLS, (c + 1) * FFN_COLS)
        a_scr[SUBLANES:SUBLANES + T, cols] = _dot(u, wup_ref[:, cols])
        a = convw_ref[FFN_CONV - 1:FFN_CONV, cols] * a_scr[SUBLANES:SUBLANES + T, cols]
        for k in range(FFN_CONV - 1):
            r0 = SUBLANES - (FFN_CONV - 1) + k
            a = a + convw_ref[k:k + 1, cols] * a_scr[r0:r0 + T, cols]
        gate = _dot(u, wgate_ref[:, cols])
        z_scr[:, cols] = (_silu(a) * gate).astype(BF16)
        yield
    a_scr[0:SUBLANES, :] = a_scr[T:T + SUBLANES, :]

    y = x1 + _dot(z_scr[...], wdown_ref[...])
    out_ref[...] = _rmsnorm(y, gfin_ref[...])


def _ffn_kernel(T, G, x_ref, mixed_ref, wout_ref, gffn_ref, wup_ref, wgate_ref, convw_ref, wdown_ref, gfin_ref,
                h1_meta_ref, out_ref, a_scr, z_scr, carry_scr):
    b = pl.program_id(0)
    t = pl.program_id(1)

    @pl.when((b == 0) & (t == 0))
    def _meta_carry():
        tail = h1_meta_ref[CHUNK - SUBLANES:CHUNK, :]
        carry_scr[...] = _dot(_rmsnorm(tail, gffn_ref[...]).astype(BF16), wup_ref[...])

    @pl.when(t == 0)
    def _load_state():
        for g in range(G):
            a_scr[g, 0:SUBLANES, :] = carry_scr[...]

    w_refs = (wout_ref, gffn_ref, wup_ref, wgate_ref, convw_ref, wdown_ref, gfin_ref)
    _trace_alternately([_ffn_tile(T, x_ref.at[g], mixed_ref.at[g], out_ref.at[g], w_refs, a_scr.at[g],
                                  z_scr.at[g]) for g in range(G)])


def _ffn_call(x3, mixed, weights, h1_meta, T, G):
    B, L, _ = x3.shape
    n_t = L // T
    wout, gffn, wup, wgate, convw, wdown, gfin = weights
    const2 = lambda b, t: (0, 0)
    resident = dict(pipeline_mode=pl.Buffered(1))
    in_specs = [
        pl.BlockSpec((G, T, D_MODEL), lambda b, t: (b, t, 0)),
        pl.BlockSpec((G, T, D_MIX), lambda b, t: (b, t, 0)),
        pl.BlockSpec(wout.shape, const2, **resident),
        pl.BlockSpec(gffn.shape, const2),
        pl.BlockSpec(wup.shape, const2, **resident),
        pl.BlockSpec(wgate.shape, const2, **resident),
        pl.BlockSpec((None,) + convw.shape[1:], lambda b, t: (0, 0, 0)),
        pl.BlockSpec(wdown.shape, const2, **resident),
        pl.BlockSpec(gfin.shape, const2),
        pl.BlockSpec(h1_meta.shape, const2),
    ]
    return pl.pallas_call(
        functools.partial(_ffn_kernel, T, G),
        out_shape=jax.ShapeDtypeStruct((B, L, D_MODEL), F32),
        grid=(B // G, n_t),
        in_specs=in_specs,
        out_specs=pl.BlockSpec((G, T, D_MODEL), lambda b, t: (b, t, 0)),
        scratch_shapes=[pltpu.VMEM((G, T + SUBLANES, D_FF), F32), pltpu.VMEM((G, T, D_FF), BF16),
                        pltpu.VMEM((SUBLANES, D_FF), F32)],
        compiler_params=pltpu.CompilerParams(
            dimension_semantics=("arbitrary", "arbitrary"),
            vmem_limit_bytes=VMEM_LIMIT_BYTES),
        name="ffn",
    )(x3, mixed, wout, gffn, wup, wgate, convw, wdown, gfin, h1_meta)


def kernel(x, meta_tokens, norm_mix_g, w_in, ml_conv_w, ml_b_i, ml_b_f, ml_norm_g, rt_norm_g, w_out,
           norm_ffn_g, w_up, w_gate, ffn_conv_w, w_down, norm_final_g):
    B, L, D = x.shape
    assert D == D_MODEL and L % TILE_MIX == 0 and L % TILE_FFN == 0
    assert w_in.shape[0] == 1, "single-layer block"

    w_in_t = jnp.transpose(w_in[0])
    assert w_in_t.shape[0] == 2 * D_GROUP + 2 * HEADS
    bias_col = jnp.concatenate([ml_b_i[0], ml_b_f[0]])[:, None]
    gmix, mlg, rtg = norm_mix_g[0][None, :], ml_norm_g[0][None, :], rt_norm_g[0][None, :]

    tables = _rope_tables(CHUNK - TILE_MIX, L + TILE_MIX)

    h1_meta, wml, wrt, wg, wout, *state = _mixer_prologue_call(
        meta_tokens.astype(F32), gmix, w_in_t, ml_conv_w[0], bias_col, mlg, rtg, w_out[0], tables,
        table_block=TILE_MIX // CHUNK - 1)
    mixer_w = (gmix, wml, wrt, wg, ml_conv_w[0], bias_col, mlg, rtg)

    G = BATCH_GROUP if B % BATCH_GROUP == 0 else 1
    mixed, wup, wgate, wdown = _mixer_call(x, mixer_w, tables, tuple(state), TILE_MIX, G,
                                           cast_weights=(w_up[0], w_gate[0], w_down[0]))
    ffn_w = (wout, norm_ffn_g[0][None, :], wup, wgate, ffn_conv_w, wdown, norm_final_g[None, :])
    return _ffn_call(x, mixed, ffn_w, h1_meta, TILE_FFN, G)
```

```python
import functools
import math

import jax
import jax.numpy as jnp
from jax import lax
from jax.experimental import pallas as pl
from jax.experimental.pallas import tpu as pltpu

F32 = jnp.float32
BF16 = jnp.bfloat16

D_MODEL = 1024
N_META = 16
CHUNK = 64
N_PAD = CHUNK - N_META
EPS = 1e-6
NEG = -1e30
HEADS = 4
DQK = 128
DV = 256
ML_CONV = 4
GATE_CAP = 15.0
ROPE_BASE = 10000.0
D_MIX = 2 * HEADS * DV
D_FF = 2816
FFN_CONV = 3
LANES = 128
SUBLANES = 8
BF16_SUBLANES = 16

OFF_Q = 0
OFF_K = OFF_Q + HEADS * DQK
OFF_V = OFF_K + HEADS * DQK
OFF_G = OFF_V + HEADS * DV
D_GROUP = OFF_G + HEADS * DV

TILE_MIX = 256
TILE_FFN = 512
BATCH_GROUP = 2
FFN_COLS = 256
VMEM_LIMIT_BYTES = 56 * 1024 * 1024
PROLOGUE_VMEM_LIMIT_BYTES = 60 * 1024 * 1024

LOG_GAMMA = tuple(math.log1p(-(2.0 ** -(5.0 + h))) for h in range(HEADS))


def _dot(a, b):
    return jnp.dot(a, b, preferred_element_type=F32)


def _dot_nt(a, b):
    return lax.dot_general(a, b, (((1,), (1,)), ((), ())), preferred_element_type=F32)


def _sigmoid(x):
    return 0.5 + 0.5 * jnp.tanh(0.5 * x)


def _silu(x):
    hx = 0.5 * x
    return hx + hx * jnp.tanh(hx)


def _rmsnorm(x, g):
    ms = jnp.mean(x * x, axis=-1, keepdims=True)
    return x * lax.rsqrt(ms + EPS) * g


def _head_norm(h, g):
    mu = jnp.mean(h, axis=-1, keepdims=True)
    c = h - mu
    var = jnp.mean(c * c, axis=-1, keepdims=True)
    return c * lax.rsqrt(var + EPS) * g


def _cumsum_lanes_mxu(x):
    r, n = x.shape
    hi = x.astype(BF16).astype(F32)
    r1 = x - hi
    mid = r1.astype(BF16).astype(F32)
    lo = (r1 - mid).astype(BF16).astype(F32)
    jsrc = lax.broadcasted_iota(jnp.int32, (n, n), 0)
    jdst = lax.broadcasted_iota(jnp.int32, (n, n), 1)
    tri = jnp.where(jsrc <= jdst, 1.0, 0.0).astype(BF16)
    p = _dot(jnp.concatenate([hi, mid, lo], axis=0).astype(BF16), tri)
    return p[0:r] + p[r:2 * r] + p[2 * r:3 * r]


def _trace_alternately(stage_generators):
    live = list(stage_generators)
    while live:
        still = []
        for gen in live:
            try:
                next(gen)
                still.append(gen)
            except StopIteration:
                pass
        live = still


def _rope_kernel(pos0, n_blocks, cos_ref, sin_ref):
    half = DQK // 2
    lane1 = lax.broadcasted_iota(jnp.int32, (1, LANES), 1)
    pair = jnp.where(lane1 < half, lane1, lane1 - half).astype(F32)
    invf = jnp.exp(pair * (-2.0 * math.log(ROPE_BASE) / DQK))
    lane = lax.broadcasted_iota(jnp.int32, (CHUNK, LANES), 1)
    off = lax.broadcasted_iota(jnp.int32, (CHUNK, LANES), 0).astype(F32) * invf
    cos_o, sin_o = jnp.cos(off), jnp.sin(off)
    base = ((lax.broadcasted_iota(jnp.int32, (n_blocks, LANES), 0) * CHUNK + pos0).astype(F32) * invf)
    cos_b, sin_b = jnp.cos(base), jnp.sin(base)
    for a in range(n_blocks):
        cb = cos_b[a:a + 1, :]
        sb = sin_b[a:a + 1, :]
        rows = slice(a * CHUNK, (a + 1) * CHUNK)
        cos_ref[rows, :] = cb * cos_o - sb * sin_o
        s = sb * cos_o + cb * sin_o
        sin_ref[rows, :] = jnp.where(lane < half, -s, s)


def _rope_tables(pos0, n_rows):
    assert pos0 % CHUNK == 0 and n_rows % CHUNK == 0
    out = jax.ShapeDtypeStruct((n_rows, LANES), F32)
    return pl.pallas_call(
        functools.partial(_rope_kernel, pos0, n_rows // CHUNK),
        out_shape=(out, out),
        name="rope_tables",
    )()


def _mixer_tile(T, has_pad, x_ref, h1_ref, w_refs, tab_refs, state_refs, qk_scr, mixed_scr):
    gmix_ref, wml_ref, wrt_ref, wg_ref, convw_ref, bias_ref, mlg_ref, rtg_ref, wout_ref = w_refs
    cos_ref, sin_ref, dmat_scr, qd_scr, kd_scr = tab_refs
    c_scr, n_scr, m_scr, s_scr = state_refs

    x = x_ref[...]
    u = _rmsnorm(x, gmix_ref[...]).astype(BF16)

    def conv_silu(c0):
        cols = slice(c0, c0 + DQK)
        acc = convw_ref[ML_CONV - 1:ML_CONV, cols] * qk_scr[SUBLANES:SUBLANES + T, cols]
        for k in range(ML_CONV - 1):
            r0 = SUBLANES - (ML_CONV - 1) + k
            acc = acc + convw_ref[k:k + 1, cols] * qk_scr[r0:r0 + T, cols]
        return _silu(acc)

    g_pre = _dot(u, wg_ref[...])
    g8 = g_pre.T[0:SUBLANES, :] + bias_ref[...]
    row8 = lax.broadcasted_iota(jnp.int32, (SUBLANES, T), 0)
    lane8 = lax.broadcasted_iota(jnp.int32, (SUBLANES, T), 1)
    li8 = GATE_CAP * jnp.tanh(g8 / GATE_CAP)
    if has_pad:
        li8 = jnp.where(lane8 >= N_PAD, li8, NEG)
    lf8 = jnp.minimum(g8, 0.0) - jnp.log1p(jnp.exp(-jnp.abs(g8)))
    bc8 = _cumsum_lanes_mxu(jnp.where(row8 >= HEADS, lf8, 0.0))
    r8 = jnp.where(row8 < HEADS, li8 - pltpu.roll(bc8, HEADS, 0), bc8)
    col = jnp.concatenate([r8, jnp.zeros((LANES - SUBLANES, T), F32)], axis=0).T
    be8 = jnp.sum(jnp.where(lane8 == T - 1, bc8, 0.0), axis=1, keepdims=True)

    qk_scr[SUBLANES:SUBLANES + T, :] = _dot(u, wml_ref[:, OFF_Q:OFF_V])

    ii = lax.broadcasted_iota(jnp.int32, (T, T), 0)
    jj = lax.broadcasted_iota(jnp.int32, (T, T), 1)
    causal = jj <= ii
    if has_pad:
        valid_col = (lax.broadcasted_iota(jnp.int32, (T, 1), 0) >= N_PAD).astype(F32)

    yield
    rq_all = _dot(u, wrt_ref[:, OFF_Q:OFF_K])
    rk_all = _dot(u, wrt_ref[:, OFF_K:OFF_V])
    cos_t = cos_ref[...]
    sin_t = sin_ref[...]
    for h in range(HEADS):
        hs = slice(h * DQK, (h + 1) * DQK)
        tq = rq_all[:, hs]
        tk = rk_all[:, hs]
        rq = tq * cos_t + pltpu.roll(tq, DQK // 2, 1) * sin_t
        rk = (tk * cos_t + pltpu.roll(tk, DQK // 2, 1) * sin_t) * (DQK ** -0.5)
        if has_pad:
            rk = rk * valid_col
        rv = _dot(u, wrt_ref[:, OFF_V + h * DV:OFF_V + (h + 1) * DV]).astype(BF16)
        yield
        scores = _dot_nt(rq.astype(BF16), rk.astype(BF16)) * dmat_scr[h]
        hr = (_dot(scores.astype(BF16), rv)
              + _dot((rq * qd_scr[h]).astype(BF16), s_scr[h].astype(BF16)))
        s_scr[h] = (math.exp(LOG_GAMMA[h] * T) * s_scr[h]
                    + _dot((rk * kd_scr[h]).T.astype(BF16), rv))
        yield
        vs = slice(h * DV, (h + 1) * DV)
        g_gate = _dot(u, wrt_ref[:, OFF_G + h * DV:OFF_G + (h + 1) * DV])
        y = _head_norm(hr, rtg_ref[:, vs]) * _silu(g_gate)
        mixed_scr[:, HEADS * DV + h * DV:HEADS * DV + (h + 1) * DV] = y.astype(BF16)
        yield

    for h in range(HEADS):
        q = conv_silu(OFF_Q + h * DQK)
        k = conv_silu(OFF_K + h * DQK) * (DQK ** -0.5)
        v = _dot(u, wml_ref[:, OFF_V + h * DV:OFF_V + (h + 1) * DV]).astype(BF16)
        qb = q.astype(BF16)
        yield
        a_col = col[:, h:h + 1]
        b_col = col[:, HEADS + h:HEADS + h + 1]
        a_row = r8[h:h + 1, :]
        m_h = m_scr[HEADS + h:HEADS + h + 1, 0:1]
        be_h = be8[HEADS + h:HEADS + h + 1, :]

        log_d = jnp.where(causal, b_col + a_row, -jnp.inf)
        log_inter = b_col + m_h
        m_row = jnp.maximum(log_inter, jnp.max(log_d, axis=1, keepdims=True))
        w_intra = jnp.exp(log_d - m_row)
        w_inter = jnp.exp(log_inter - m_row)
        s = _dot_nt(qb, k.astype(BF16)) * w_intra
        n_h = n_scr[h:h + 1, :]
        den = (jnp.sum(s, axis=1, keepdims=True)
               + w_inter * jnp.sum(q * n_h, axis=1, keepdims=True))
        num = _dot(s.astype(BF16), v) + w_inter * _dot(qb, c_scr[h].astype(BF16))
        hh = num / jnp.maximum(jnp.abs(den), jnp.exp(-m_row))

        yield
        lte = be_h + a_col
        m_new = jnp.maximum(be_h + m_h, jnp.max(lte, axis=0, keepdims=True))
        w_src = jnp.exp(lte - m_new)
        decay = jnp.exp(be_h + m_h - m_new)
        kw = k * w_src
        c_scr[h] = decay * c_scr[h] + _dot(kw.T.astype(BF16), v)
        n_scr[h:h + 1, :] = decay * n_h + jnp.sum(kw, axis=0, keepdims=True)
        m_scr[HEADS + h:HEADS + h + 1, :] = jnp.broadcast_to(m_new, (1, LANES))
        yield

        vs = slice(h * DV, (h + 1) * DV)
        o_gate = _dot(u, wml_ref[:, OFF_G + h * DV:OFF_G + (h + 1) * DV])
        y = _head_norm(hh, mlg_ref[:, vs]) * _sigmoid(o_gate)
        mixed_scr[:, vs] = y.astype(BF16)
        yield

    qk_scr[0:SUBLANES, :] = qk_scr[T:T + SUBLANES, :]

    h1_ref[...] = x + _dot(mixed_scr[...], wout_ref[...])


def _build_decay_tables(T, dmat_scr, qd_scr, kd_scr):
    ii = lax.broadcasted_iota(jnp.int32, (T, T), 0)
    jj = lax.broadcasted_iota(jnp.int32, (T, T), 1)
    rel = (ii - jj).astype(F32)
    ri = lax.broadcasted_iota(jnp.int32, (T, LANES), 0).astype(F32)
    for h in range(HEADS):
        lg = LOG_GAMMA[h]
        dmat_scr[h] = jnp.where(rel >= 0.0, jnp.exp(lg * jnp.maximum(rel, 0.0)), 0.0)
        qd_scr[h] = jnp.exp(lg * (ri + 1.0))
        kd_scr[h] = jnp.exp(lg * (T - 1.0 - ri))


def _mixer_kernel(T, G, n_cast, *refs):
    x_ref = refs[0]
    w_refs = refs[1:10]
    cos_ref, sin_ref, c0_ref, n0_ref, m0_ref, s0_ref, cc0_ref = refs[10:17]
    cast_in = refs[17:17 + n_cast]
    h1_ref = refs[17 + n_cast]
    cast_out = refs[18 + n_cast:18 + 2 * n_cast]
    c_scr, n_scr, m_scr, s_scr, qk_scr, dmat_scr, qd_scr, kd_scr, mixed_scr = refs[18 + 2 * n_cast:]

    b = pl.program_id(0)
    t = pl.program_id(1)

    @pl.when(t == 0)
    def _load_state():
        for g in range(G):
            c_scr[g] = c0_ref[...]
            n_scr[g] = n0_ref[...]
            m_scr[g] = m0_ref[...]
            s_scr[g] = s0_ref[...]
            qk_scr[g, 0:SUBLANES, :] = cc0_ref[...]

    @pl.when((b == 0) & (t == 0))
    def _tables():
        _build_decay_tables(T, dmat_scr, qd_scr, kd_scr)

    for src, dst in zip(cast_in, cast_out):
        dst[...] = src[...].astype(BF16)

    _trace_alternately([
        _mixer_tile(T, False, x_ref.at[g], h1_ref.at[g], w_refs,
                    (cos_ref, sin_ref, dmat_scr, qd_scr, kd_scr),
                    (c_scr.at[g], n_scr.at[g], m_scr.at[g], s_scr.at[g]),
                    qk_scr.at[g], mixed_scr.at[g]) for g in range(G)])


def _mixer_prologue_kernel(x_ref, gmix_ref, w_in_t_ref, convw_ref, bias_ref, mlg_ref, rtg_ref,
                           w_out_ref, cos_ref, sin_ref,
                           h1_ref, wml_ref, wrt_ref, wg_ref, wout_ref, c_ref, n_ref, m_ref, s_ref, cc_ref,
                           qk_scr, dmat_scr, qd_scr, kd_scr, mixed_scr, x_scr):
    x_scr[...] = jnp.concatenate([jnp.zeros((N_PAD, D_MODEL), F32), x_ref[...]], axis=0)
    blk = DV
    for c in range(D_GROUP // blk):
        cols = slice(c * blk, (c + 1) * blk)
        wml_ref[:, cols] = w_in_t_ref[c * blk:(c + 1) * blk, :].T.astype(BF16)
        r0 = D_GROUP + 2 * HEADS + c * blk
        wrt_ref[:, cols] = w_in_t_ref[r0:r0 + blk, :].T.astype(BF16)
    gates_t = jnp.concatenate([w_in_t_ref[D_GROUP:D_GROUP + 2 * HEADS, :],
                               jnp.zeros((LANES - 2 * HEADS, D_MODEL), F32)], axis=0)
    wg_ref[...] = gates_t.T.astype(BF16)
    wout_ref[...] = w_out_ref[...].astype(BF16)

    c_ref[...] = jnp.zeros(c_ref.shape, F32)
    n_ref[...] = jnp.zeros(n_ref.shape, F32)
    m_ref[...] = jnp.full(m_ref.shape, NEG, F32)
    s_ref[...] = jnp.zeros(s_ref.shape, F32)
    qk_scr[0:SUBLANES, :] = jnp.zeros((SUBLANES, qk_scr.shape[1]), F32)
    _build_decay_tables(CHUNK, dmat_scr, qd_scr, kd_scr)

    w_refs = (gmix_ref, wml_ref, wrt_ref, wg_ref, convw_ref, bias_ref, mlg_ref, rtg_ref, wout_ref)
    _trace_alternately([
        _mixer_tile(CHUNK, True, x_scr, h1_ref, w_refs, (cos_ref, sin_ref, dmat_scr, qd_scr, kd_scr),
                    (c_ref, n_ref, m_ref, s_ref), qk_scr, mixed_scr)])
    cc_ref[...] = qk_scr[0:SUBLANES, :]


def _mixer_prologue_call(meta_tokens, gmix, w_in_t, convw, bias_col, mlg, rtg, w_out, tables, table_block):
    cos_t, sin_t = tables
    T = CHUNK
    ins = (meta_tokens, gmix, w_in_t, convw, bias_col, mlg, rtg, w_out)
    whole = lambda a: pl.BlockSpec(a.shape, lambda i, nd=a.ndim: (0,) * nd)
    in_specs = [whole(a) for a in ins] + [pl.BlockSpec((T, LANES), lambda i: (table_block, 0))] * 2
    f32 = lambda *shape: jax.ShapeDtypeStruct(shape, F32)
    bf16 = lambda *shape: jax.ShapeDtypeStruct(shape, BF16)
    out_shape = (f32(T, D_MODEL), bf16(D_MODEL, D_GROUP), bf16(D_MODEL, D_GROUP), bf16(D_MODEL, LANES),
                 bf16(D_MIX, D_MODEL), f32(HEADS, DQK, DV), f32(SUBLANES, LANES), f32(SUBLANES, LANES),
                 f32(HEADS, DQK, DV), f32(SUBLANES, 2 * HEADS * DQK))
    scratch = [
        pltpu.VMEM((T + SUBLANES, 2 * HEADS * DQK), F32),
        pltpu.VMEM((HEADS, T, T), F32),
        pltpu.VMEM((HEADS, T, LANES), F32),
        pltpu.VMEM((HEADS, T, LANES), F32),
        pltpu.VMEM((T, D_MIX), BF16),
        pltpu.VMEM((T, D_MODEL), F32),
    ]
    return pl.pallas_call(
        _mixer_prologue_kernel,
        out_shape=out_shape,
        grid=(1,),
        in_specs=in_specs,
        out_specs=tuple(pl.BlockSpec(o.shape, lambda i, nd=len(o.shape): (0,) * nd) for o in out_shape),
        scratch_shapes=scratch,
        compiler_params=pltpu.CompilerParams(dimension_semantics=("arbitrary",),
                                             vmem_limit_bytes=PROLOGUE_VMEM_LIMIT_BYTES),
        name="mixer_prologue",
    )(*ins, cos_t, sin_t)


def _mixer_call(x3, weights, tables, state, T, G, cast_weights=()):
    B, L, _ = x3.shape
    n_t = L // T
    gmix, wml, wrt, wg, convw, bias_col, mlg, rtg, wout = weights
    cos_t, sin_t = tables
    c0, n0, m0, s0, cc0 = state
    const2 = lambda b, t: (0, 0)
    const3 = lambda b, t: (0, 0, 0)
    in_specs = [
        pl.BlockSpec((G, T, D_MODEL), lambda b, t: (b, t, 0)),
        pl.BlockSpec(gmix.shape, const2),
        pl.BlockSpec(wml.shape, const2),
        pl.BlockSpec(wrt.shape, const2),
        pl.BlockSpec(wg.shape, const2),
        pl.BlockSpec(convw.shape, const2),
        pl.BlockSpec(bias_col.shape, const2),
        pl.BlockSpec(mlg.shape, const2),
        pl.BlockSpec(rtg.shape, const2),
        pl.BlockSpec(wout.shape, const2),
        pl.BlockSpec((T, LANES), lambda b, t: (t + 1, 0)),
        pl.BlockSpec((T, LANES), lambda b, t: (t + 1, 0)),
        pl.BlockSpec(c0.shape, const3),
        pl.BlockSpec(n0.shape, const2),
        pl.BlockSpec(m0.shape, const2),
        pl.BlockSpec(s0.shape, const3),
        pl.BlockSpec(cc0.shape, const2),
    ]
    out_shape = [jax.ShapeDtypeStruct((B, L, D_MODEL), F32)]
    out_specs = [pl.BlockSpec((G, T, D_MODEL), lambda b, t: (b, t, 0))]
    cast_map = lambda b, t: (jnp.where(b == 0, t, n_t - 1), 0)
    for wf in cast_weights:
        rows = wf.shape[0] // n_t
        assert rows * n_t == wf.shape[0] and rows % BF16_SUBLANES == 0
        in_specs.append(pl.BlockSpec((rows, wf.shape[1]), cast_map))
        out_shape.append(jax.ShapeDtypeStruct(wf.shape, BF16))
        out_specs.append(pl.BlockSpec((rows, wf.shape[1]), cast_map))
    scratch = [
        pltpu.VMEM((G,) + c0.shape, F32),
        pltpu.VMEM((G,) + n0.shape, F32),
        pltpu.VMEM((G,) + m0.shape, F32),
        pltpu.VMEM((G,) + s0.shape, F32),
        pltpu.VMEM((G, T + SUBLANES, 2 * HEADS * DQK), F32),
        pltpu.VMEM((HEADS, T, T), F32),
        pltpu.VMEM((HEADS, T, LANES), F32),
        pltpu.VMEM((HEADS, T, LANES), F32),
        pltpu.VMEM((G, T, D_MIX), BF16),
    ]
    return pl.pallas_call(
        functools.partial(_mixer_kernel, T, G, len(cast_weights)),
        out_shape=tuple(out_shape),
        grid=(B // G, n_t),
        in_specs=in_specs,
        out_specs=tuple(out_specs),
        scratch_shapes=scratch,
        compiler_params=pltpu.CompilerParams(
            dimension_semantics=("arbitrary", "arbitrary"),
            vmem_limit_bytes=VMEM_LIMIT_BYTES),
        name="mixer",
    )(x3, gmix, wml, wrt, wg, convw, bias_col, mlg, rtg, wout, cos_t, sin_t, c0, n0, m0, s0, cc0,
      *cast_weights)


def _ffn_tile(T, h1_ref, out_ref, w_refs, a_scr, z_scr):
    gffn_ref, wup_ref, wgate_ref, convw_ref, wdown_ref, gfin_ref = w_refs
    x1 = h1_ref[...]
    u = _rmsnorm(x1, gffn_ref[...]).astype(BF16)
    for c in range(D_FF // FFN_COLS):
        cols = slice(c * FFN_COLS, (c + 1) * FFN_COLS)
        a_scr[SUBLANES:SUBLANES + T, cols] = _dot(u, wup_ref[:, cols])
        a = convw_ref[FFN_CONV - 1:FFN_CONV, cols] * a_scr[SUBLANES:SUBLANES + T, cols]
        for k in range(FFN_CONV - 1):
            r0 = SUBLANES - (FFN_CONV - 1) + k
            a = a + convw_ref[k:k + 1, cols] * a_scr[r0:r0 + T, cols]
        gate = _dot(u, wgate_ref[:, cols])
        z_scr[:, cols] = (_silu(a) * gate).astype(BF16)
        yield
    a_scr[0:SUBLANES, :] = a_scr[T:T + SUBLANES, :]

    y = x1 + _dot(z_scr[...], wdown_ref[...])
    out_ref[...] = _rmsnorm(y, gfin_ref[...])


def _ffn_kernel(T, G, h1_ref, gffn_ref, wup_ref, wgate_ref, convw_ref, wdown_ref, gfin_ref, h1_meta_ref,
                out_ref, a_scr, z_scr, carry_scr):
    b = pl.program_id(0)
    t = pl.program_id(1)

    @pl.when((b == 0) & (t == 0))
    def _meta_carry():
        tail = h1_meta_ref[CHUNK - SUBLANES:CHUNK, :]
        carry_scr[...] = _dot(_rmsnorm(tail, gffn_ref[...]).astype(BF16), wup_ref[...])

    @pl.when(t == 0)
    def _load_state():
        for g in range(G):
            a_scr[g, 0:SUBLANES, :] = carry_scr[...]

    w_refs = (gffn_ref, wup_ref, wgate_ref, convw_ref, wdown_ref, gfin_ref)
    _trace_alternately([_ffn_tile(T, h1_ref.at[g], out_ref.at[g], w_refs, a_scr.at[g], z_scr.at[g])
                        for g in range(G)])


def _ffn_call(h1, weights, h1_meta, T, G):
    B, L, _ = h1.shape
    n_t = L // T
    gffn, wup, wgate, convw, wdown, gfin = weights
    const2 = lambda b, t: (0, 0)
    resident = dict(pipeline_mode=pl.Buffered(1))
    in_specs = [
        pl.BlockSpec((G, T, D_MODEL), lambda b, t: (b, t, 0)),
        pl.BlockSpec(gffn.shape, const2),
        pl.BlockSpec(wup.shape, const2, **resident),
        pl.BlockSpec(wgate.shape, const2, **resident),
        pl.BlockSpec((None,) + convw.shape[1:], lambda b, t: (0, 0, 0)),
        pl.BlockSpec(wdown.shape, const2, **resident),
        pl.BlockSpec(gfin.shape, const2),
        pl.BlockSpec(h1_meta.shape, const2),
    ]
    return pl.pallas_call(
        functools.partial(_ffn_kernel, T, G),
        out_shape=jax.ShapeDtypeStruct((B, L, D_MODEL), F32),
        grid=(B // G, n_t),
        in_specs=in_specs,
        out_specs=pl.BlockSpec((G, T, D_MODEL), lambda b, t: (b, t, 0)),
        scratch_shapes=[pltpu.VMEM((G, T + SUBLANES, D_FF), F32), pltpu.VMEM((G, T, D_FF), BF16),
                        pltpu.VMEM((SUBLANES, D_FF), F32)],
        compiler_params=pltpu.CompilerParams(
            dimension_semantics=("arbitrary", "arbitrary"),
            vmem_limit_bytes=VMEM_LIMIT_BYTES),
        name="ffn",
    )(h1, gffn, wup, wgate, convw, wdown, gfin, h1_meta)


def kernel(x, meta_tokens, norm_mix_g, w_in, ml_conv_w, ml_b_i, ml_b_f, ml_norm_g, rt_norm_g, w_out,
           norm_ffn_g, w_up, w_gate, ffn_conv_w, w_down, norm_final_g):
    B, L, D = x.shape
    assert D == D_MODEL and L % TILE_MIX == 0 and L % TILE_FFN == 0
    assert w_in.shape[0] == 1, "single-layer block"

    w_in_t = jnp.transpose(w_in[0])
    assert w_in_t.shape[0] == 2 * D_GROUP + 2 * HEADS
    bias_col = jnp.concatenate([ml_b_i[0], ml_b_f[0]])[:, None]
    gmix, mlg, rtg = norm_mix_g[0][None, :], ml_norm_g[0][None, :], rt_norm_g[0][None, :]

    tables = _rope_tables(CHUNK - TILE_MIX, L + TILE_MIX)

    h1_meta, wml, wrt, wg, wout, *state = _mixer_prologue_call(
        meta_tokens.astype(F32), gmix, w_in_t, ml_conv_w[0], bias_col, mlg, rtg, w_out[0], tables,
        table_block=TILE_MIX // CHUNK - 1)
    mixer_w = (gmix, wml, wrt, wg, ml_conv_w[0], bias_col, mlg, rtg, wout)

    G = BATCH_GROUP if B % BATCH_GROUP == 0 else 1
    h1, wup, wgate, wdown = _mixer_call(x, mixer_w, tables, tuple(state), TILE_MIX, G,
                                        cast_weights=(w_up[0], w_gate[0], w_down[0]))
    ffn_w = (norm_ffn_g[0][None, :], wup, wgate, ffn_conv_w, wdown, norm_final_g[None, :])
    return _ffn_call(h1, ffn_w, h1_meta, TILE_FFN, G)
```
